```python
import jax
import jax.numpy as jnp
from jax import lax
import numpy as np

D_MODEL = 2048
BATCH = 1
SEQ = 16384
DEPTH = 2

D_BRANCH = 1024
N_BRANCH = 3
D_A = D_BRANCH
H_A = 16
BW_A = D_A // H_A
CONV_W = 4
LRU_C = 8.0
DH_B = 128
H_B = D_BRANCH // DH_B
D_B = H_B * DH_B
Q_BLOCK = 128
N_C = 64
H_C = D_BRANCH // N_C
D_C = H_C * N_C
LORA_W = 64
LORA_A = 64
N_SHIFT = 3 * D_C + LORA_W + LORA_A
NORM_EPS = 1e-6
GN_EPS = 64e-5
COL_A = 2 * D_A
COL_B = 4 * D_B + H_B
COL_GATE_C = D_C
COL_MERGE = N_BRANCH * D_MODEL
N_IN = COL_A + COL_B + N_SHIFT + COL_GATE_C + COL_MERGE

kernel_name = 'hybrid_rglru_fox_rwkv7_block'


def _rmsnorm(x, g):
    xf = x.astype(jnp.float32)
    y = xf * lax.rsqrt(jnp.mean(xf * xf, -1, keepdims=True) + NORM_EPS)
    return (y * g.astype(jnp.float32)).astype(x.dtype)


def _split(p, sizes):
    out, off = [], 0
    for s in sizes:
        out.append(p[..., off:off + s])
        off += s
    return out


def _rglru_branch(xa, ga, conv_w, conv_b, gate_w, gate_b, lam):
    f32 = jnp.float32
    B, S, _ = xa.shape
    xc = lax.conv_general_dilated(xa, conv_w[:, None, :], window_strides=(1,), padding=[(CONV_W - 1, 0)],
                                  dimension_numbers=('NWC', 'WIO', 'NWC'), feature_group_count=D_A) + conv_b
    xb = xc.reshape(B, S, H_A, BW_A)
    gates = jnp.einsum('bshi,ghij->gbshj', xb, gate_w).reshape(2, B, S, D_A) + gate_b[:, None, None, :]
    r = jax.nn.sigmoid(gates[0].astype(f32))
    i = jax.nn.sigmoid(gates[1].astype(f32))
    log_a = -LRU_C * r * jax.nn.softplus(-lam.astype(f32))
    a = jnp.exp(log_a)
    mult = jnp.sqrt(-jnp.expm1(2.0 * log_a))
    mult = jnp.where((jnp.arange(S) == 0)[None, :, None], 1.0, mult)
    b = mult * i * xc.astype(f32)

    def combine(lhs, rhs):
        a1, b1 = lhs
        a2, b2 = rhs
        return a1 * a2, a2 * b1 + b2

    _, h = lax.associative_scan(combine, (a, b), axis=1)
    return h.astype(xa.dtype) * jax.nn.silu(ga)


def _fox_branch(q, k, v, f_logit, gb, b_f):
    f32 = jnp.float32
    B, S, _ = q.shape
    nb = S // Q_BLOCK
    qh = (q.astype(f32) * DH_B ** -0.5).reshape(B, S, H_B, DH_B)
    kh = k.astype(f32).reshape(B, S, H_B, DH_B)
    vh = v.astype(f32).reshape(B, S, H_B, DH_B)
    c = jnp.cumsum(jax.nn.log_sigmoid(f_logit.astype(f32) + b_f.astype(f32)), axis=1)
    c_bhs = jnp.transpose(c, (0, 2, 1))
    q_blocks = jnp.transpose(qh.reshape(B, nb, Q_BLOCK, H_B, DH_B), (1, 0, 2, 3, 4))
    c_blocks = jnp.transpose(c.reshape(B, nb, Q_BLOCK, H_B), (1, 0, 3, 2))
    k_pos = jnp.arange(S)

    def block(args):
        qb, cb, bi = args
        q_pos = bi * Q_BLOCK + jnp.arange(Q_BLOCK)
        s = jnp.einsum('bqhd,bkhd->bhqk', qb, kh)
        s = s + cb[..., :, None] - c_bhs[:, :, None, :]
        s = jnp.where(k_pos[None, :] <= q_pos[:, None], s, -jnp.inf)
        p = jax.nn.softmax(s, axis=-1)
        return jnp.einsum('bhqk,bkhd->bqhd', p, vh)

    o = lax.map(block, (q_blocks, c_blocks, jnp.arange(nb)))
    o = jnp.transpose(o, (1, 0, 2, 3, 4)).reshape(B, S, D_B)
    return o.astype(q.dtype) * jax.nn.silu(gb)


def _rwkv7_branch(cr, ck, cv, cwd, cad, gc, w0, w_up, a0, a_up, k_k, k_a, r_k, ln_w, ln_b):
    f32 = jnp.float32
    B, S, _ = cr.shape
    heads = lambda t: t.reshape(B, S, H_C, N_C)
    w_log = -jax.nn.softplus(-(w0 + jnp.tanh(cwd) @ w_up).astype(f32)) - 0.5
    decay = jnp.exp(-jnp.exp(w_log))
    a = jax.nn.sigmoid((a0 + cad @ a_up).astype(f32))
    k = ck.astype(f32)
    kk = heads(k * k_k.astype(f32))
    kk = kk * lax.rsqrt(jnp.maximum(jnp.sum(kk * kk, -1, keepdims=True), 1e-24))
    k = heads(k * (1.0 + (a - 1.0) * k_a.astype(f32)))
    r = heads(cr.astype(f32))
    v = heads(cv.astype(f32))
    decay = heads(decay)
    a = heads(a)

    def step(state, inp):
        r_t, w_t, k_t, v_t, kk_t, a_t = inp
        sa = jnp.einsum('bhvk,bhk->bhv', state, -kk_t)
        state = (state * w_t[:, :, None, :] + sa[..., None] * (kk_t * a_t)[:, :, None, :]
                 + v_t[..., None] * k_t[:, :, None, :])
        return state, jnp.einsum('bhvk,bhk->bhv', state, r_t)

    xs = tuple(jnp.swapaxes(t, 0, 1) for t in (r, decay, k, v, kk, a))
    state0 = jnp.zeros((B, H_C, N_C, N_C), f32)
    _, y = lax.scan(step, state0, xs)
    y = jnp.swapaxes(y, 0, 1)
    mu = jnp.mean(y, -1, keepdims=True)
    var = jnp.mean(jnp.square(y - mu), -1, keepdims=True)
    y = ((y - mu) * lax.rsqrt(var + GN_EPS)).reshape(B, S, D_C) * ln_w.astype(f32) + ln_b.astype(f32)
    bonus = jnp.sum(r * k * r_k.astype(f32), -1, keepdims=True) * v
    y = y + bonus.reshape(B, S, D_C)
    return y.astype(gc.dtype) * jax.nn.silu(gc)


def setup_inputs(seed: int = 0) -> dict:
    key = jax.random.key(seed)
    ks = jax.random.split(key, 24)
    f32 = jnp.float32
    nrm = lambda k, shape, scale: jax.random.normal(k, shape, f32) * scale
    x = nrm(ks[0], (BATCH, SEQ, D_MODEL), 1.0)
    pre_norm_w = 1.0 + nrm(ks[1], (DEPTH, D_MODEL), 0.02)
    post_norm_w = 1.0 + nrm(ks[2], (DEPTH, D_MODEL), 0.02)
    w_in = nrm(ks[3], (DEPTH, D_MODEL, N_IN), D_MODEL ** -0.5)
    b_merge = nrm(ks[4], (DEPTH, N_BRANCH, D_MODEL), 0.02)
    conv_w = nrm(ks[5], (DEPTH, CONV_W, D_A), CONV_W ** -0.5)
    conv_b = nrm(ks[6], (DEPTH, D_A), 0.02)
    lru_gate_w = nrm(ks[7], (DEPTH, 2, H_A, BW_A, BW_A), BW_A ** -0.5)
    lru_gate_b = nrm(ks[8], (DEPTH, 2, D_A), 0.02)
    u = jax.random.uniform(ks[9], (DEPTH, D_A), f32, 0.9, 0.999)
    p = u ** (1.0 / LRU_C)
    lru_lambda = jnp.log(p) - jnp.log1p(-p)
    fox_b_f = jax.random.uniform(ks[10], (DEPTH, H_B), f32, 1.0, 4.0)
    rwkv_mu = jax.random.uniform(ks[11], (DEPTH, N_SHIFT), f32, 0.0, 1.0)
    rwkv_w0 = jax.random.uniform(ks[12], (DEPTH, D_C), f32, -6.0, -1.0)
    rwkv_w_up = nrm(ks[13], (DEPTH, LORA_W, D_C), 0.5 * LORA_W ** -0.5)
    rwkv_a0 = nrm(ks[14], (DEPTH, D_C), 0.1)
    rwkv_a_up = nrm(ks[15], (DEPTH, LORA_A, D_C), LORA_A ** -0.5)
    rwkv_k_k = 0.85 + nrm(ks[16], (DEPTH, D_C), 0.02)
    rwkv_k_a = 1.0 + nrm(ks[17], (DEPTH, D_C), 0.02)
    rwkv_r_k = nrm(ks[18], (DEPTH, H_C, N_C), 0.1)
    rwkv_ln_w = 1.0 + nrm(ks[19], (DEPTH, D_C), 0.02)
    rwkv_ln_b = nrm(ks[20], (DEPTH, D_C), 0.02)
    w_branch = nrm(ks[21], (DEPTH, N_BRANCH, D_BRANCH, D_MODEL), D_BRANCH ** -0.5)
    w_out = nrm(ks[22], (DEPTH, D_MODEL, D_MODEL), D_MODEL ** -0.5)
    return {'x': x, 'pre_norm_w': pre_norm_w, 'post_norm_w': post_norm_w, 'w_in': w_in, 'b_merge': b_merge,
            'conv_w': conv_w, 'conv_b': conv_b, 'lru_gate_w': lru_gate_w, 'lru_gate_b': lru_gate_b,
            'lru_lambda': lru_lambda, 'fox_b_f': fox_b_f, 'rwkv_mu': rwkv_mu, 'rwkv_w0': rwkv_w0,
            'rwkv_w_up': rwkv_w_up, 'rwkv_a0': rwkv_a0, 'rwkv_a_up': rwkv_a_up, 'rwkv_k_k': rwkv_k_k,
            'rwkv_k_a': rwkv_k_a, 'rwkv_r_k': rwkv_r_k, 'rwkv_ln_w': rwkv_ln_w, 'rwkv_ln_b': rwkv_ln_b,
            'w_branch': w_branch, 'w_out': w_out}


def reference(x, pre_norm_w, post_norm_w, w_in, b_merge, conv_w, conv_b, lru_gate_w, lru_gate_b, lru_lambda,
              fox_b_f, rwkv_mu, rwkv_w0, rwkv_w_up, rwkv_a0, rwkv_a_up, rwkv_k_k, rwkv_k_a, rwkv_r_k,
              rwkv_ln_w, rwkv_ln_b, w_branch, w_out):
    B, S, _ = x.shape
    for l in range(DEPTH):
        h = _rmsnorm(x, pre_norm_w[l])
        p = h @ w_in[l]
        pa, pb, pc, gc, mg = _split(p, (COL_A, COL_B, N_SHIFT, COL_GATE_C, COL_MERGE))
        xa, ga = _split(pa, (D_A, D_A))
        q, k, v, fl, gb = _split(pb, (D_B, D_B, D_B, H_B, D_B))
        pc_prev = jnp.pad(pc, ((0, 0), (1, 0), (0, 0)))[:, :-1]
        pc = pc + (pc_prev - pc) * rwkv_mu[l]
        cr, ck, cv, cwd, cad = _split(pc, (D_C, D_C, D_C, LORA_W, LORA_A))
        ya = _rglru_branch(xa, ga, conv_w[l], conv_b[l], lru_gate_w[l], lru_gate_b[l], lru_lambda[l])
        yb = _fox_branch(q, k, v, fl, gb, fox_b_f[l])
        yc = _rwkv7_branch(cr, ck, cv, cwd, cad, gc, rwkv_w0[l], rwkv_w_up[l], rwkv_a0[l], rwkv_a_up[l],
                           rwkv_k_k[l], rwkv_k_a[l], rwkv_r_k[l], rwkv_ln_w[l], rwkv_ln_b[l])
        g = jax.nn.sigmoid(mg.reshape(B, S, N_BRANCH, D_MODEL) + b_merge[l])
        m = (g[:, :, 0] * (ya @ w_branch[l, 0]) + g[:, :, 1] * (yb @ w_branch[l, 1])
             + g[:, :, 2] * (yc @ w_branch[l, 2]))
        x = x + _rmsnorm(m @ w_out[l], post_norm_w[l])
    return x
```

```python
import functools

import jax
import jax.numpy as jnp
from jax import lax
from jax.experimental import pallas as pl
from jax.experimental.pallas import tpu as pltpu

F32 = jnp.float32
BF16 = jnp.bfloat16

D_MODEL = 2048
D_BR = 1024
H_A, BW_A = 16, 64
CONV_W = 4
LRU_C = 8.0
H_B, DH_B = 8, 128
H_C, N_C = 16, 64
LORA = 64
NORM_EPS = 1e-6
GN_EPS = 64e-5
LANE = 128
SUB = 8
CHUNK = 64
N_PAIR = H_C // 2
NEG = -1e30

PC_XA, PC_GA, PC_GB, PC_GC, PC_CR, PC_CK, PC_CV = 0, 8, 16, 24, 32, 40, 48
PC_WA, PC_FL = 56, 57
P_COLS = 60 * LANE

VMEM_LIMIT = 48 * 1024 * 1024


def _cparams(sem):
    return pltpu.CompilerParams(dimension_semantics=sem, vmem_limit_bytes=VMEM_LIMIT)


def _sigmoid(x):
    return 1.0 / (1.0 + jnp.exp(-x))


def _silu(x):
    return x * _sigmoid(x)


def _softplus(z):
    return jnp.maximum(z, 0.0) + jnp.log1p(jnp.exp(-jnp.abs(z)))


def _norm_kernel(x_ref, g_ref, o_ref):
    x = x_ref[...]
    y = x * lax.rsqrt(jnp.mean(x * x, axis=-1, keepdims=True) + NORM_EPS)
    o_ref[...] = (y * g_ref[...]).astype(o_ref.dtype)


def _rmsnorm_bf16(x, g):
    S = x.shape[0]
    tm = min(512, S)
    return pl.pallas_call(
        _norm_kernel,
        grid=(S // tm,),
        in_specs=[pl.BlockSpec((tm, D_MODEL), lambda i: (i, 0)),
                  pl.BlockSpec((1, D_MODEL), lambda i: (0, 0))],
        out_specs=pl.BlockSpec((tm, D_MODEL), lambda i: (i, 0)),
        out_shape=jax.ShapeDtypeStruct((S, D_MODEL), BF16),
        compiler_params=_cparams(("parallel",)),
        name="rmsnorm",
    )(x, g.reshape(1, D_MODEL))


def _proj_kernel(h_ref, w_ref, s_ref, o_ref):
    acc = jnp.dot(h_ref[...], w_ref[...], preferred_element_type=F32)
    o_ref[...] = (acc * s_ref[...]).astype(o_ref.dtype)


def _proj(h, w, colscale, out_dtype, tn):
    S, K = h.shape
    N = w.shape[1]
    tm = min(1024, S)
    return pl.pallas_call(
        _proj_kernel,
        grid=(S // tm, N // tn),
        in_specs=[pl.BlockSpec((tm, K), lambda i, j: (i, 0)),
                  pl.BlockSpec((K, tn), lambda i, j: (0, j)),
                  pl.BlockSpec((1, tn), lambda i, j: (0, j))],
        out_specs=pl.BlockSpec((tm, tn), lambda i, j: (i, j)),
        out_shape=jax.ShapeDtypeStruct((S, N), out_dtype),
        compiler_params=_cparams(("parallel", "parallel")),
        name="in_proj",
    )(h, w, colscale)


def _rglru_kernel(xa_ref, ga_ref, cw_ref, cb_ref, wg_ref, gbias_ref, lam_ref, o_ref,
                  xe_ref, hc_ref, a_ref, b_ref, *, T):
    i = pl.program_id(0)

    @pl.when(i == 0)
    def _():
        xe_ref[0:SUB, :] = jnp.zeros((SUB, D_BR), F32)
        hc_ref[...] = jnp.zeros_like(hc_ref)

    xe_ref[SUB:SUB + T, :] = xa_ref[...]
    cw = cw_ref[...]
    xc = cb_ref[...] + xe_ref[SUB:SUB + T, :] * cw[3:4]
    for j in range(CONV_W - 1):
        xc = xc + xe_ref[SUB - 3 + j:SUB - 3 + j + T, :] * cw[j:j + 1]
    xe_ref[0:SUB, :] = xe_ref[T:T + SUB, :]

    gates = jnp.dot(xc.astype(BF16), wg_ref[...], preferred_element_type=F32) + gbias_ref[...]
    r = _sigmoid(gates[:, :D_BR])
    ig = _sigmoid(gates[:, D_BR:])
    log_a = (-LRU_C) * r * _softplus(-lam_ref[...])
    a = jnp.exp(log_a)
    mult = jnp.sqrt(-jnp.tanh(log_a) * (1.0 + a * a))
    rows = lax.broadcasted_iota(jnp.int32, (T, D_BR), 0)
    mult = jnp.where((rows + i * T) == 0, 1.0, mult)
    b = mult * ig * xc

    rmod = rows & (SUB - 1)
    for d in (1, 2, 4):
        keep = rmod >= d
        a_s = pltpu.roll(a, d, 0)
        b_s = pltpu.roll(b, d, 0)
        b = jnp.where(keep, a * b_s + b, b)
        a = jnp.where(keep, a * a_s, a)
    a_ref[...] = a
    b_ref[...] = b
    carry = hc_ref[...]
    for g in range(T // SUB):
        sl = slice(g * SUB, (g + 1) * SUB)
        hg = b_ref[sl, :] + a_ref[sl, :] * carry
        b_ref[sl, :] = hg
        carry = jnp.broadcast_to(hg[SUB - 1:SUB, :], (SUB, D_BR))
    hc_ref[...] = carry
    o_ref[...] = (b_ref[...] * _silu(ga_ref[...])).astype(o_ref.dtype)


def _rglru(P, conv_w, conv_b, wg, gbias, lam):
    S = P.shape[0]
    T = min(256, S)
    nb = D_BR // LANE
    return pl.pallas_call(
        functools.partial(_rglru_kernel, T=T),
        grid=(S // T,),
        in_specs=[pl.BlockSpec((T, D_BR), lambda i: (i, PC_XA // nb)),
                  pl.BlockSpec((T, D_BR), lambda i: (i, PC_GA // nb)),
                  pl.BlockSpec((CONV_W, D_BR), lambda i: (0, 0)),
                  pl.BlockSpec((1, D_BR), lambda i: (0, 0)),
                  pl.BlockSpec((D_BR, 2 * D_BR), lambda i: (0, 0)),
                  pl.BlockSpec((1, 2 * D_BR), lambda i: (0, 0)),
                  pl.BlockSpec((1, D_BR), lambda i: (0, 0))],
        out_specs=pl.BlockSpec((T, D_BR), lambda i: (i, 0)),
        out_shape=jax.ShapeDtypeStruct((S, D_BR), BF16),
        scratch_shapes=[pltpu.VMEM((T + SUB, D_BR), F32),
                        pltpu.VMEM((SUB, D_BR), F32),
                        pltpu.VMEM((T, D_BR), F32),
                        pltpu.VMEM((T, D_BR), F32)],
        compiler_params=_cparams(("arbitrary",)),
        name="rglru",
    )(P, P, conv_w, conv_b, wg, gbias, lam)


def _fox_c_kernel(fl_ref, bf_ref, o_ref, carry_ref, *, T):
    i = pl.program_id(0)

    @pl.when(i == 0)
    def _():
        carry_ref[...] = jnp.zeros_like(carry_ref)

    z = fl_ref[...] + bf_ref[...]
    lf = jnp.minimum(z, 0.0) - jnp.log1p(jnp.exp(-jnp.abs(z)))
    rr = lax.broadcasted_iota(jnp.int32, (T, T), 0)
    cc = lax.broadcasted_iota(jnp.int32, (T, T), 1)
    upper = (rr <= cc).astype(F32)
    cT = lax.dot_general(lf, upper, (((0,), (0,)), ((), ())),
                         precision=lax.Precision.HIGHEST, preferred_element_type=F32)
    cT = cT + carry_ref[...][:, 0:1]
    carry_ref[...] = jnp.broadcast_to(cT[:, T - 1:T], (LANE, LANE))
    o_ref[...] = cT[0:H_B, :]


def _fox_c(P, bf_pad):
    S = P.shape[0]
    T = min(512, S)
    return pl.pallas_call(
        functools.partial(_fox_c_kernel, T=T),
        grid=(S // T,),
        in_specs=[pl.BlockSpec((T, LANE), lambda i: (i, PC_FL)),
                  pl.BlockSpec((1, LANE), lambda i: (0, 0))],
        out_specs=pl.BlockSpec((H_B, T), lambda i: (0, i)),
        out_shape=jax.ShapeDtypeStruct((H_B, S), F32),
        scratch_shapes=[pltpu.VMEM((LANE, LANE), F32)],
        compiler_params=_cparams(("arbitrary",)),
        name="fox_cumgate",
    )(P, bf_pad)


def _fox_kernel(cq_ref, q_ref, k_ref, v_ref, c_ref, gb_ref, o_ref, *, tq):
    h = pl.program_id(0)
    i = pl.program_id(1)
    q = q_ref[...]
    c_q0 = cq_ref[h, i]

    def scores(j):
        ks = pl.multiple_of(j * tq, tq)
        kc = k_ref[pl.ds(ks, tq), :]
        vc = v_ref[pl.ds(ks, tq), :]
        s = lax.dot_general(q, kc, (((1,), (1,)), ((), ())), preferred_element_type=F32)
        s = s + (c_q0 - c_ref[:, pl.ds(ks, tq)])
        return s, vc

    def update(carry, s, vc):
        m, l, acc = carry
        m_new = jnp.maximum(m, jnp.max(s, axis=-1, keepdims=True))
        alpha = jnp.exp(m - m_new)
        p = jnp.exp(s - m_new)
        l = alpha * l + jnp.sum(p, axis=-1, keepdims=True)
        acc = alpha * acc + jnp.dot(p.astype(BF16), vc, preferred_element_type=F32)
        return m_new, l, acc

    def body(j, carry):
        s, vc = scores(j)
        return update(carry, s, vc)

    init = (jnp.full((tq, 1), NEG, F32), jnp.zeros((tq, 1), F32), jnp.zeros((tq, DH_B), F32))
    carry = lax.fori_loop(0, i, body, init)
    s, vc = scores(i)
    rr = lax.broadcasted_iota(jnp.int32, (tq, tq), 0)
    cc = lax.broadcasted_iota(jnp.int32, (tq, tq), 1)
    s = jnp.where(cc <= rr, s, NEG)
    _, l, acc = update(carry, s, vc)
    o_ref[...] = (acc / l * _silu(gb_ref[...])).astype(o_ref.dtype)


def _fox(qkv, c_row, P):
    S = qkv.shape[0]
    tq = min(512, S)
    n_i = S // tq
    cq = c_row[:, ::tq]
    c3 = c_row.reshape(H_B, 1, S)
    return pl.pallas_call(
        functools.partial(_fox_kernel, tq=tq),
        grid=(H_B, n_i),
        in_specs=[pl.BlockSpec(memory_space=pltpu.SMEM),
                  pl.BlockSpec((tq, DH_B), lambda h, i: (i, h)),
                  pl.BlockSpec((S, DH_B), lambda h, i: (0, H_B + h)),
                  pl.BlockSpec((S, DH_B), lambda h, i: (0, 2 * H_B + h)),
                  pl.BlockSpec((None, 1, S), lambda h, i: (h, 0, 0)),
                  pl.BlockSpec((tq, DH_B), lambda h, i: (i, PC_GB + h))],
        out_specs=pl.BlockSpec((tq, DH_B), lambda h, i: (i, h)),
        out_shape=jax.ShapeDtypeStruct((S, D_BR), BF16),
        compiler_params=_cparams(("parallel", "parallel")),
        name="fox_attn",
    )(cq, qkv, qkv, qkv, c3, P)


def _head_sum(x):
    rr = lax.broadcasted_iota(jnp.int32, (LANE, LANE), 0) // N_C
    cc = lax.broadcasted_iota(jnp.int32, (LANE, LANE), 1) // N_C
    ones_bd = (rr == cc).astype(F32)
    outs = []
    for p in range(N_PAIR):
        outs.append(jnp.dot(x[:, p * LANE:(p + 1) * LANE], ones_bd,
                            precision=lax.Precision.HIGHEST, preferred_element_type=F32))
    return jnp.concatenate(outs, axis=1)


def _rwkv_prep_kernel(cr_ref, ck_ref, cv_ref, wa_ref, pr_ref, pk_ref, pv_ref, pwa_ref,
                      mu_ref, muwa_ref, w0_ref, wup_ref, a0_ref, aup_ref, kk_ref, ka_ref,
                      r_out, k_out, v_out, kkn_out, b_out, g_out, *, T):
    i = pl.program_id(0)
    live = (i > 0).astype(F32)

    def shift_mix(cur_ref, prev_ref, mu):
        cur = cur_ref[...]
        rows = lax.broadcasted_iota(jnp.int32, cur.shape, 0)
        first = jnp.broadcast_to(prev_ref[SUB - 1:SUB, :] * live, cur.shape)
        prev = jnp.where(rows == 0, first, pltpu.roll(cur, 1, 0))
        return cur + (prev - cur) * mu

    r = shift_mix(cr_ref, pr_ref, mu_ref[0:1, :])
    kraw = shift_mix(ck_ref, pk_ref, mu_ref[1:2, :])
    v = shift_mix(cv_ref, pv_ref, mu_ref[2:3, :])
    wa = shift_mix(wa_ref, pwa_ref, muwa_ref[...])
    zw = w0_ref[...] + jnp.dot(jnp.tanh(wa).astype(BF16), wup_ref[...], preferred_element_type=F32)
    za = a0_ref[...] + jnp.dot(wa.astype(BF16), aup_ref[...], preferred_element_type=F32)
    g = (-0.6065306597126334) * _sigmoid(zw)
    a = _sigmoid(za)
    kk = kraw * kk_ref[...]
    ss = _head_sum(kk * kk)
    kkn = kk * lax.rsqrt(jnp.maximum(ss, 1e-24))
    r_out[...] = r
    k_out[...] = kraw * (1.0 + (a - 1.0) * ka_ref[...])
    v_out[...] = v
    kkn_out[...] = kkn
    b_out[...] = kkn * a
    g_out[...] = g


def _rwkv_prep(P, mu3, muwa, w0, wup, a0, aup, k_k, k_a):
    S = P.shape[0]
    T = min(256, S)
    nb = D_BR // LANE
    prev = lambda i: jnp.maximum(i * (T // SUB) - 1, 0)
    row = lambda n: pl.BlockSpec((1, n), lambda i: (0, 0))
    big = pl.BlockSpec((T, D_BR), lambda i: (i, 0))
    return pl.pallas_call(
        functools.partial(_rwkv_prep_kernel, T=T),
        grid=(S // T,),
        in_specs=[pl.BlockSpec((T, D_BR), lambda i: (i, PC_CR // nb)),
                  pl.BlockSpec((T, D_BR), lambda i: (i, PC_CK // nb)),
                  pl.BlockSpec((T, D_BR), lambda i: (i, PC_CV // nb)),
                  pl.BlockSpec((T, LANE), lambda i: (i, PC_WA)),
                  pl.BlockSpec((SUB, D_BR), lambda i: (prev(i), PC_CR // nb)),
                  pl.BlockSpec((SUB, D_BR), lambda i: (prev(i), PC_CK // nb)),
                  pl.BlockSpec((SUB, D_BR), lambda i: (prev(i), PC_CV // nb)),
                  pl.BlockSpec((SUB, LANE), lambda i: (prev(i), PC_WA)),
                  pl.BlockSpec((3, D_BR), lambda i: (0, 0)),
                  row(LANE), row(D_BR),
                  pl.BlockSpec((LANE, D_BR), lambda i: (0, 0)),
                  row(D_BR),
                  pl.BlockSpec((LANE, D_BR), lambda i: (0, 0)),
                  row(D_BR), row(D_BR)],
        out_specs=[big] * 6,
        out_shape=[jax.ShapeDtypeStruct((S, D_BR), F32)] * 6,
        compiler_params=_cparams(("parallel",)),
        name="rwkv_prep",
    )(P, P, P, P, P, P, P, P, mu3, muwa, w0, wup, a0, aup, k_k, k_a)


def _mm(a, b):
    return jnp.dot(a.astype(BF16), b.astype(BF16), preferred_element_type=F32)


def _mm_nt(a, b):
    return lax.dot_general(a.astype(BF16), b.astype(BF16), (((1,), (1,)), ((), ())),
                           preferred_element_type=F32)


def _mm_tn(a, b):
    return lax.dot_general(a.astype(BF16), b.astype(BF16), (((0,), (0,)), ((), ())),
                           preferred_element_type=F32)


def _rwkv_chunk_kernel(r_ref, k_ref, v_ref, kkn_ref, b_ref, g_ref,
                       mt_ref, nt_ref, rh_ref, yl_ref):
    L = CHUNK
    g = g_ref[...]
    rr = lax.broadcasted_iota(jnp.int32, (L, L), 0)
    cc = lax.broadcasted_iota(jnp.int32, (L, L), 1)
    G = jnp.dot((cc <= rr).astype(F32), g, precision=lax.Precision.HIGHEST,
                preferred_element_type=F32)
    Ge = G - g
    Gm = G[L // 2 - 1:L // 2, :]
    GL = G[L - 1:L, :]
    na = -kkn_ref[...]
    bv = b_ref[...]
    kv = k_ref[...]
    rv = r_ref[...]
    al = na * jnp.exp(Ge)
    al_m = na * jnp.exp(Ge - Gm)
    e_b = jnp.exp(Gm - G)
    be_m = bv * e_b
    ka_m = kv * e_b
    rho = rv * jnp.exp(G)
    rho_m = rv * jnp.exp(G - Gm)
    e_l = jnp.exp(GL - G)
    b_hat = bv * e_l
    k_hat = kv * e_l
    d_row = jnp.exp(GL)
    vm = v_ref[...]

    lane1 = lax.broadcasted_iota(jnp.int32, (L, LANE), 1)
    lo1 = lane1 < N_C
    lane2 = lax.broadcasted_iota(jnp.int32, (L, 2 * LANE), 1)
    lo2 = (lane2 & (LANE - 1)) < N_C
    r4 = lax.broadcasted_iota(jnp.int32, (4 * L, L), 0)
    c4 = lax.broadcasted_iota(jnp.int32, (4 * L, L), 1)
    t4 = r4 & (L - 1)
    tri4 = c4 < t4 + jnp.where(r4 < 2 * L, 0, 1)
    rb = lax.broadcasted_iota(jnp.int32, (LANE, LANE), 0)
    cb = lax.broadcasted_iota(jnp.int32, (LANE, LANE), 1)
    bd = (rb // N_C) == (cb // N_C)
    eye = rb == cb

    for p in range(N_PAIR):
        sl = slice(p * LANE, (p + 1) * LANE)
        alm_p, rhm_p = al_m[:, sl], rho_m[:, sl]
        lhs4 = jnp.concatenate([jnp.where(lo1, alm_p, 0.0), jnp.where(lo1, 0.0, alm_p),
                                jnp.where(lo1, rhm_p, 0.0), jnp.where(lo1, 0.0, rhm_p)], axis=0)
        a_b = jnp.where(tri4, _mm_nt(lhs4, be_m[:, sl]), 0.0)
        a_k = jnp.where(tri4, _mm_nt(lhs4, ka_m[:, sl]), 0.0)
        vm_p = vm[:, sl]
        akv = jnp.where(lo1, _mm(a_k[0:L], vm_p), _mm(a_k[L:2 * L], vm_p))
        x0 = jnp.concatenate([al[:, sl], akv], axis=1)
        xs = []
        for hh in range(2):
            apow = a_b[hh * L:(hh + 1) * L]
            x = x0 + _mm(apow, x0)
            for _ in range(5):
                apow = _mm(apow, apow)
                x = x + _mm(apow, x)
            xs.append(x)
        pq = jnp.where(lo2, xs[0], xs[1])
        rbs = [_mm(a_b[(2 + hh) * L:(3 + hh) * L], pq) for hh in range(2)]
        rks = [_mm(a_k[(2 + hh) * L:(3 + hh) * L], vm_p) for hh in range(2)]
        rh_ref[:, sl] = (rho[:, sl] + jnp.where(lo1, rbs[0][:, :LANE], rbs[1][:, :LANE])).astype(rh_ref.dtype)
        yl_ref[:, sl] = jnp.where(lo1, rbs[0][:, LANE:] + rks[0], rbs[1][:, LANE:] + rks[1])
        g1 = _mm_tn(b_hat[:, sl], pq)
        g2 = _mm_tn(k_hat[:, sl], vm_p)
        d_p = jnp.broadcast_to(d_row[:, sl], (LANE, LANE))
        mt_ref[0, p] = (jnp.where(eye, d_p, 0.0) + jnp.where(bd, g1[:, :LANE], 0.0)).astype(mt_ref.dtype)
        nt_ref[0, p] = jnp.where(bd, g1[:, LANE:] + g2, 0.0)


def _rwkv_chunks(r, k, v, kkn, b, g):
    S = r.shape[0]
    nc = S // CHUNK
    big = pl.BlockSpec((CHUNK, D_BR), lambda c: (c, 0))
    mat = pl.BlockSpec((1, N_PAIR, LANE, LANE), lambda c: (c, 0, 0, 0))
    return pl.pallas_call(
        _rwkv_chunk_kernel,
        grid=(nc,),
        in_specs=[big] * 6,
        out_specs=[mat, mat, big, big],
        out_shape=[jax.ShapeDtypeStruct((nc, N_PAIR, LANE, LANE), BF16),
                   jax.ShapeDtypeStruct((nc, N_PAIR, LANE, LANE), F32),
                   jax.ShapeDtypeStruct((S, D_BR), BF16),
                   jax.ShapeDtypeStruct((S, D_BR), F32)],
        compiler_params=_cparams(("parallel",)),
        name="rwkv_chunks",
    )(r, k, v, kkn, b, g)


def _rwkv_seq_kernel(mt_ref, nt_ref, rh_ref, yl_ref, r_ref, k_ref, v_ref, gc_ref,
                     rk_ref, lnw_ref, lnb_ref, o_ref, z_ref, y_ref, *, NCH):
    i = pl.program_id(0)

    @pl.when(i == 0)
    def _():
        z_ref[...] = jnp.zeros_like(z_ref)

    for p in range(N_PAIR):
        sl = slice(p * LANE, (p + 1) * LANE)
        z = z_ref[p]
        for c in range(NCH):
            rows = slice(c * CHUNK, (c + 1) * CHUNK)
            zb = z.astype(BF16)
            y_ref[rows, sl] = jnp.dot(rh_ref[rows, sl], zb, preferred_element_type=F32) + yl_ref[rows, sl]
            z = jnp.dot(mt_ref[c, p], zb, preferred_element_type=F32) + nt_ref[c, p]
        z_ref[p] = z

    y = y_ref[...]
    inv_n = 1.0 / N_C
    mu = _head_sum(y) * inv_n
    yc = y - mu
    var = _head_sum(yc * yc) * inv_n
    yn = yc * lax.rsqrt(var + GN_EPS) * lnw_ref[...] + lnb_ref[...]
    vv = v_ref[...]
    bonus = _head_sum(r_ref[...] * k_ref[...] * rk_ref[...]) * vv
    o_ref[...] = ((yn + bonus) * _silu(gc_ref[...])).astype(o_ref.dtype)


def _rwkv_seq(mt, nt, rh, yl, r, k, v, P, r_k, ln_w, ln_b):
    S = r.shape[0]
    T = min(256, S)
    nch = T // CHUNK
    nb = D_BR // LANE
    big = pl.BlockSpec((T, D_BR), lambda i: (i, 0))
    mat = pl.BlockSpec((nch, N_PAIR, LANE, LANE), lambda i: (i, 0, 0, 0))
    row = pl.BlockSpec((1, D_BR), lambda i: (0, 0))
    return pl.pallas_call(
        functools.partial(_rwkv_seq_kernel, NCH=nch),
        grid=(S // T,),
        in_specs=[mat, mat, big, big, big, big, big,
                  pl.BlockSpec((T, D_BR), lambda i: (i, PC_GC // nb)),
                  row, row, row],
        out_specs=big,
        out_shape=jax.ShapeDtypeStruct((S, D_BR), BF16),
        scratch_shapes=[pltpu.VMEM((N_PAIR, LANE, LANE), F32),
                        pltpu.VMEM((T, D_BR), F32)],
        compiler_params=_cparams(("arbitrary",)),
        name="rwkv_seq",
    )(mt, nt, rh, yl, r, k, v, P, r_k, ln_w, ln_b)


def _merge_kernel(h_ref, wg0_ref, wg1_ref, wg2_ref, bm_ref, ya_ref, yb_ref, yc_ref,
                  wb0_ref, wb1_ref, wb2_ref, o_ref):
    h = h_ref[...]
    bm = bm_ref[...]
    acc = None
    for n, (wg_ref, y_ref, wb_ref) in enumerate(((wg0_ref, ya_ref, wb0_ref),
                                                  (wg1_ref, yb_ref, wb1_ref),
                                                  (wg2_ref, yc_ref, wb2_ref))):
        gate = _sigmoid(jnp.dot(h, wg_ref[...], preferred_element_type=F32) + bm[n:n + 1, :])
        term = gate * jnp.dot(y_ref[...], wb_ref[...], preferred_element_type=F32)
        acc = term if acc is None else acc + term
    o_ref[...] = acc.astype(o_ref.dtype)


def _merge(h, w_mg, b_merge, ya, yb, yc, w_br):
    S = h.shape[0]
    tm = min(512, S)
    tn = 512
    nj = D_MODEL // tn
    ysp = pl.BlockSpec((tm, D_BR), lambda i, j: (i, 0))
    return pl.pallas_call(
        _merge_kernel,
        grid=(S // tm, nj),
        in_specs=[pl.BlockSpec((tm, D_MODEL), lambda i, j: (i, 0)),
                  pl.BlockSpec((D_MODEL, tn), lambda i, j: (0, j)),
                  pl.BlockSpec((D_MODEL, tn), lambda i, j: (0, nj + j)),
                  pl.BlockSpec((D_MODEL, tn), lambda i, j: (0, 2 * nj + j)),
                  pl.BlockSpec((3, tn), lambda i, j: (0, j)),
                  ysp, ysp, ysp,
                  pl.BlockSpec((None, D_BR, tn), lambda i, j: (0, 0, j)),
                  pl.BlockSpec((None, D_BR, tn), lambda i, j: (1, 0, j)),
                  pl.BlockSpec((None, D_BR, tn), lambda i, j: (2, 0, j))],
        out_specs=pl.BlockSpec((tm, tn), lambda i, j: (i, j)),
        out_shape=jax.ShapeDtypeStruct((S, D_MODEL), BF16),
        compiler_params=_cparams(("parallel", "parallel")),
        name="merge",
    )(h, w_mg, w_mg, w_mg, b_merge, ya, yb, yc, w_br, w_br, w_br)


def _out_kernel(m_ref, w_ref, x_ref, g_ref, o_ref):
    y = jnp.dot(m_ref[...], w_ref[...], preferred_element_type=F32)
    yn = y * lax.rsqrt(jnp.mean(y * y, axis=-1, keepdims=True) + NORM_EPS)
    o_ref[...] = x_ref[...] + yn * g_ref[...]


def _out_proj(m, w_out, x, g):
    S = m.shape[0]
    tm = min(512, S)
    blk = pl.BlockSpec((tm, D_MODEL), lambda i: (i, 0))
    return pl.pallas_call(
        _out_kernel,
        grid=(S // tm,),
        in_specs=[blk,
                  pl.BlockSpec((D_MODEL, D_MODEL), lambda i: (0, 0)),
                  blk,
                  pl.BlockSpec((1, D_MODEL), lambda i: (0, 0))],
        out_specs=blk,
        out_shape=jax.ShapeDtypeStruct((S, D_MODEL), F32),
        compiler_params=_cparams(("parallel",)),
        name="out_proj",
    )(m, w_out, x, g.reshape(1, D_MODEL))


def _layer_weights(w_in, lru_gate_w, rwkv_w_up, rwkv_a_up):
    o = 0
    xa_w, ga_w = w_in[:, o:o + D_BR], w_in[:, o + D_BR:o + 2 * D_BR]
    o += 2 * D_BR
    qkv_w = w_in[:, o:o + 3 * D_BR]
    fl_w = w_in[:, o + 3 * D_BR:o + 3 * D_BR + H_B]
    gb_w = w_in[:, o + 3 * D_BR + H_B:o + 4 * D_BR + H_B]
    o += 4 * D_BR + H_B
    pc_w = w_in[:, o:o + 3 * D_BR + 2 * LORA]
    o += 3 * D_BR + 2 * LORA
    gc_w = w_in[:, o:o + D_BR]
    o += D_BR
    mg_w = w_in[:, o:]
    pad = jnp.zeros((D_MODEL, P_COLS - (PC_FL * LANE + H_B)), F32)
    w_p = jnp.concatenate([xa_w, ga_w, gb_w, gc_w, pc_w, fl_w, pad], axis=1).astype(BF16)
    eye = jnp.eye(H_A, dtype=F32)
    wg = jnp.einsum('ghij,hk->ghikj', lru_gate_w, eye).reshape(2, D_BR, D_BR)
    wg = jnp.concatenate([wg[0], wg[1]], axis=1).astype(BF16)
    z = jnp.zeros((LORA, D_BR), F32)
    wup = jnp.concatenate([rwkv_w_up, z], axis=0).astype(BF16)
    aup = jnp.concatenate([z, rwkv_a_up], axis=0).astype(BF16)
    return w_p, qkv_w.astype(BF16), mg_w.astype(BF16), wg, wup, aup


def kernel(x, pre_norm_w, post_norm_w, w_in, b_merge, conv_w, conv_b, lru_gate_w, lru_gate_b, lru_lambda,
           fox_b_f, rwkv_mu, rwkv_w0, rwkv_w_up, rwkv_a0, rwkv_a_up, rwkv_k_k, rwkv_k_a, rwkv_r_k,
           rwkv_ln_w, rwkv_ln_b, w_branch, w_out):
    B, S, _ = x.shape
    assert B == 1 and S % CHUNK == 0
    depth = w_in.shape[0]
    xs = x.reshape(S, D_MODEL)
    ones_p = jnp.ones((1, P_COLS), F32)
    qscale = jnp.concatenate([jnp.full((1, D_BR), DH_B ** -0.5, F32), jnp.ones((1, 2 * D_BR), F32)], axis=1)
    row = lambda a: a.reshape(1, -1)
    for l in range(depth):
        w_p, w_qkv, w_mg, wg, wup, aup = _layer_weights(w_in[l], lru_gate_w[l], rwkv_w_up[l], rwkv_a_up[l])
        h = _rmsnorm_bf16(xs, pre_norm_w[l])
        P = _proj(h, w_p, ones_p, F32, 512)
        qkv = _proj(h, w_qkv, qscale, BF16, 512)
        ya = _rglru(P, conv_w[l], row(conv_b[l]), wg, row(lru_gate_b[l]), row(lru_lambda[l]))
        bf_pad = jnp.concatenate([fox_b_f[l], jnp.zeros((LANE - H_B,), F32)]).reshape(1, LANE)
        c_row = _fox_c(P, bf_pad)
        yb = _fox(qkv, c_row, P)
        mu = rwkv_mu[l]
        mu3 = mu[:3 * D_BR].reshape(3, D_BR)
        muwa = mu[3 * D_BR:].reshape(1, LANE)
        r, k, v, kkn, b, g = _rwkv_prep(P, mu3, muwa, row(rwkv_w0[l]), wup, row(rwkv_a0[l]), aup,
                                        row(rwkv_k_k[l]), row(rwkv_k_a[l]))
        mt, nt, rh, yl = _rwkv_chunks(r, k, v, kkn, b, g)
        yc = _rwkv_seq(mt, nt, rh, yl, r, k, v, P, row(rwkv_r_k[l]), row(rwkv_ln_w[l]), row(rwkv_ln_b[l]))
        m = _merge(h, w_mg, b_merge[l], ya, yb, yc, w_branch[l].astype(BF16))
        xs = _out_proj(m, w_out[l].astype(BF16), xs, post_norm_w[l])
    return xs.reshape(B, S, D_MODEL)
```

```python
import functools

import jax
import jax.numpy as jnp
from jax import lax
from jax.experimental import pallas as pl
from jax.experimental.pallas import tpu as pltpu

F32 = jnp.float32
BF16 = jnp.bfloat16

D_MODEL = 2048
D_BR = 1024
H_A, BW_A = 16, 64
CONV_W = 4
LRU_C = 8.0
H_B, DH_B = 8, 128
H_C, N_C = 16, 64
LORA = 64
NORM_EPS = 1e-6
GN_EPS = 64e-5
LANE = 128
SUB = 8
CHUNK = 64
N_PAIR = H_C // 2
NEG = -1e30
LOG2E = 1.4426950408889634
FOX_T = 512

PC_XA, PC_GA, PC_GB, PC_GC, PC_CR, PC_CK, PC_CV = 0, 8, 16, 24, 32, 40, 48
PC_WA, PC_FL = 56, 57
P_COLS = 60 * LANE

VMEM_LIMIT = 48 * 1024 * 1024


def _cparams(sem):
    return pltpu.CompilerParams(dimension_semantics=sem, vmem_limit_bytes=VMEM_LIMIT)


def _sigmoid(x):
    return 1.0 / (1.0 + jnp.exp(-x))


def _silu(x):
    return x * _sigmoid(x)


def _softplus(z):
    return jnp.maximum(z, 0.0) + jnp.log1p(jnp.exp(-jnp.abs(z)))


def _norm_kernel(x_ref, g_ref, o_ref):
    x = x_ref[...]
    y = x * lax.rsqrt(jnp.mean(x * x, axis=-1, keepdims=True) + NORM_EPS)
    o_ref[...] = (y * g_ref[...]).astype(o_ref.dtype)


def _rmsnorm_bf16(x, g):
    S = x.shape[0]
    tm = min(512, S)
    return pl.pallas_call(
        _norm_kernel,
        grid=(S // tm,),
        in_specs=[pl.BlockSpec((tm, D_MODEL), lambda i: (i, 0)),
                  pl.BlockSpec((1, D_MODEL), lambda i: (0, 0))],
        out_specs=pl.BlockSpec((tm, D_MODEL), lambda i: (i, 0)),
        out_shape=jax.ShapeDtypeStruct((S, D_MODEL), BF16),
        compiler_params=_cparams(("parallel",)),
        name="rmsnorm",
    )(x, g.reshape(1, D_MODEL))


def _proj_kernel(h_ref, w_ref, s_ref, o_ref):
    acc = jnp.dot(h_ref[...], w_ref[...], preferred_element_type=F32)
    o_ref[...] = (acc * s_ref[...]).astype(o_ref.dtype)


def _proj(h, w, colscale, out_dtype, tn):
    S, K = h.shape
    N = w.shape[1]
    tm = min(1024, S)
    return pl.pallas_call(
        _proj_kernel,
        grid=(S // tm, N // tn),
        in_specs=[pl.BlockSpec((tm, K), lambda i, j: (i, 0)),
                  pl.BlockSpec((K, tn), lambda i, j: (0, j)),
                  pl.BlockSpec((1, tn), lambda i, j: (0, j))],
        out_specs=pl.BlockSpec((tm, tn), lambda i, j: (i, j)),
        out_shape=jax.ShapeDtypeStruct((S, N), out_dtype),
        compiler_params=_cparams(("parallel", "parallel")),
        name="in_proj",
    )(h, w, colscale)


def _rglru_kernel(xa_ref, ga_ref, cw_ref, cb_ref, wg_ref, gbias_ref, lam_ref, o_ref,
                  xe_ref, hc_ref, a_ref, b_ref, *, T):
    i = pl.program_id(0)

    @pl.when(i == 0)
    def _():
        xe_ref[0:SUB, :] = jnp.zeros((SUB, D_BR), F32)
        hc_ref[...] = jnp.zeros_like(hc_ref)

    xe_ref[SUB:SUB + T, :] = xa_ref[...]
    cw = cw_ref[...]
    xc = cb_ref[...] + xe_ref[SUB:SUB + T, :] * cw[3:4]
    for j in range(CONV_W - 1):
        xc = xc + xe_ref[SUB - 3 + j:SUB - 3 + j + T, :] * cw[j:j + 1]
    xe_ref[0:SUB, :] = xe_ref[T:T + SUB, :]

    gates = jnp.dot(xc.astype(BF16), wg_ref[...], preferred_element_type=F32) + gbias_ref[...]
    r = _sigmoid(gates[:, :D_BR])
    ig = _sigmoid(gates[:, D_BR:])
    log_a = (-LRU_C) * r * _softplus(-lam_ref[...])
    a = jnp.exp(log_a)
    mult = jnp.sqrt(-jnp.tanh(log_a) * (1.0 + a * a))
    rows = lax.broadcasted_iota(jnp.int32, (T, D_BR), 0)
    mult = jnp.where((rows + i * T) == 0, 1.0, mult)
    b = mult * ig * xc

    rmod = rows & (SUB - 1)
    for d in (1, 2, 4):
        keep = rmod >= d
        a_s = pltpu.roll(a, d, 0)
        b_s = pltpu.roll(b, d, 0)
        b = jnp.where(keep, a * b_s + b, b)
        a = jnp.where(keep, a * a_s, a)
    a_ref[...] = a
    b_ref[...] = b
    carry = hc_ref[...]
    for g in range(T // SUB):
        sl = slice(g * SUB, (g + 1) * SUB)
        hg = b_ref[sl, :] + a_ref[sl, :] * carry
        b_ref[sl, :] = hg
        carry = jnp.broadcast_to(hg[SUB - 1:SUB, :], (SUB, D_BR))
    hc_ref[...] = carry
    o_ref[...] = (b_ref[...] * _silu(ga_ref[...])).astype(o_ref.dtype)


def _rglru(P, conv_w, conv_b, wg, gbias, lam):
    S = P.shape[0]
    T = min(256, S)
    nb = D_BR // LANE
    return pl.pallas_call(
        functools.partial(_rglru_kernel, T=T),
        grid=(S // T,),
        in_specs=[pl.BlockSpec((T, D_BR), lambda i: (i, PC_XA // nb)),
                  pl.BlockSpec((T, D_BR), lambda i: (i, PC_GA // nb)),
                  pl.BlockSpec((CONV_W, D_BR), lambda i: (0, 0)),
                  pl.BlockSpec((1, D_BR), lambda i: (0, 0)),
                  pl.BlockSpec((D_BR, 2 * D_BR), lambda i: (0, 0)),
                  pl.BlockSpec((1, 2 * D_BR), lambda i: (0, 0)),
                  pl.BlockSpec((1, D_BR), lambda i: (0, 0))],
        out_specs=pl.BlockSpec((T, D_BR), lambda i: (i, 0)),
        out_shape=jax.ShapeDtypeStruct((S, D_BR), BF16),
        scratch_shapes=[pltpu.VMEM((T + SUB, D_BR), F32),
                        pltpu.VMEM((SUB, D_BR), F32),
                        pltpu.VMEM((T, D_BR), F32),
                        pltpu.VMEM((T, D_BR), F32)],
        compiler_params=_cparams(("arbitrary",)),
        name="rglru",
    )(P, P, conv_w, conv_b, wg, gbias, lam)


def _fox_c_kernel(fl_ref, bf_ref, kx_ref, cs_ref, carry_ref, *, T):
    i = pl.program_id(0)

    @pl.when(i == 0)
    def _():
        carry_ref[...] = jnp.zeros_like(carry_ref)

    z = fl_ref[...] + bf_ref[...]
    lf = (jnp.minimum(z, 0.0) - jnp.log1p(jnp.exp(-jnp.abs(z)))) * LOG2E
    rr = lax.broadcasted_iota(jnp.int32, (T, T), 0)
    cc = lax.broadcasted_iota(jnp.int32, (T, T), 1)
    lc = jnp.dot((cc <= rr).astype(F32), lf, precision=lax.Precision.HIGHEST,
                 preferred_element_type=F32)
    carry = carry_ref[...]
    cs_ref[0] = carry
    carry_ref[...] = carry + jnp.broadcast_to(lc[T - 1:T, :], (SUB, LANE))
    x = -lc
    hi = x.astype(BF16)
    r1 = x - hi.astype(F32)
    mid = r1.astype(BF16)
    lo = (r1 - mid.astype(F32)).astype(BF16)
    pieces = jnp.concatenate([hi, mid, lo], axis=1)
    rs = lax.broadcasted_iota(jnp.int32, (3 * LANE, D_BR), 0)
    cs = lax.broadcasted_iota(jnp.int32, (3 * LANE, D_BR), 1)
    sel = (cs == (rs % LANE) * LANE + rs // LANE).astype(BF16)
    kx_ref[...] = jnp.dot(pieces, sel, preferred_element_type=F32).astype(BF16)


def _fox_c(P, bf_pad):
    S = P.shape[0]
    T = min(FOX_T, S // 2)
    return pl.pallas_call(
        functools.partial(_fox_c_kernel, T=T),
        grid=(S // T,),
        in_specs=[pl.BlockSpec((T, LANE), lambda i: (i, PC_FL)),
                  pl.BlockSpec((1, LANE), lambda i: (0, 0))],
        out_specs=[pl.BlockSpec((T, D_BR), lambda i: (i, 0)),
                   pl.BlockSpec((1, SUB, LANE), lambda i: (i, 0, 0))],
        out_shape=[jax.ShapeDtypeStruct((S, D_BR), BF16),
                   jax.ShapeDtypeStruct((S // T, SUB, LANE), F32)],
        scratch_shapes=[pltpu.VMEM((SUB, LANE), F32)],
        compiler_params=_cparams(("arbitrary",)),
        name="fox_cumgate",
    )(P, bf_pad)


def _fox_kernel(cs_ref, q_ref, k_ref, kx_ref, v_ref, gb_ref, o_ref, *, T):
    h = pl.program_id(0)
    i = pl.program_id(1)
    lane = lax.broadcasted_iota(jnp.int32, (T, DH_B), 1)
    ones3 = jnp.where(lane < 3, 1.0, 0.0).astype(BF16)
    qa = [jnp.concatenate([q_ref[u * T:(u + 1) * T, :], ones3], axis=1) for u in range(2)]
    ones_v = jnp.ones((T, DH_B), BF16)
    c_i = cs_ref[h, 2 * i]
    tri = lax.broadcasted_iota(jnp.int32, (T, T), 1) <= lax.broadcasted_iota(jnp.int32, (T, T), 0)

    def step(j, carry, modes):
        ks = pl.multiple_of(j * T, T)
        kc = jnp.concatenate([k_ref[pl.ds(ks, T), :], kx_ref[pl.ds(ks, T), :]], axis=1)
        vc = jnp.concatenate([v_ref[pl.ds(ks, T), :], ones_v], axis=1)
        d = c_i - cs_ref[h, j]
        s = [None if modes[u] == 'skip' else
             lax.dot_general(qa[u], kc, (((1,), (1,)), ((), ())), preferred_element_type=F32)
             for u in range(2)]
        out = []
        for u in range(2):
            if modes[u] == 'skip':
                out.append(carry[u])
                continue
            m, acc = carry[u]
            su = jnp.where(tri, s[u], NEG) if modes[u] == 'tri' else s[u]
            m_new = jnp.maximum(m, jnp.max(su, axis=-1, keepdims=True) + d)
            p = jnp.exp2(su - (m_new - d))
            acc = jnp.exp2(m - m_new) * acc + jnp.dot(p.astype(BF16), vc, preferred_element_type=F32)
            out.append((m_new, acc))
        return tuple(out)

    init = tuple((jnp.full((T, 1), NEG, F32), jnp.zeros((T, 2 * DH_B), F32)) for _ in range(2))
    carry = lax.fori_loop(0, 2 * i, lambda j, c: step(j, c, ('full', 'full')), init)
    carry = step(2 * i, carry, ('tri', 'full'))
    carry = step(2 * i + 1, carry, ('skip', 'tri'))
    for u in range(2):
        acc = carry[u][1]
        rows = slice(u * T, (u + 1) * T)
        o_ref[rows, :] = (acc[:, :DH_B] / acc[:, DH_B:] * _silu(gb_ref[rows, :])).astype(o_ref.dtype)


def _fox(qkv, kx, cs, P):
    S = qkv.shape[0]
    T = min(FOX_T, S // 2)
    cst = cs[:, 0, :H_B].T
    kv = lambda col0: pl.BlockSpec((S, DH_B), lambda h, i: (0, col0 + h))
    return pl.pallas_call(
        functools.partial(_fox_kernel, T=T),
        grid=(H_B, S // (2 * T)),
        in_specs=[pl.BlockSpec(memory_space=pltpu.SMEM),
                  pl.BlockSpec((2 * T, DH_B), lambda h, i: (i, h)),
                  kv(H_B), kv(0), kv(2 * H_B),
                  pl.BlockSpec((2 * T, DH_B), lambda h, i: (i, PC_GB + h))],
        out_specs=pl.BlockSpec((2 * T, DH_B), lambda h, i: (i, h)),
        out_shape=jax.ShapeDtypeStruct((S, D_BR), BF16),
        compiler_params=_cparams(("parallel", "parallel")),
        name="fox_attn",
    )(cst, qkv, qkv, kx, qkv, P)


def _head_sum(x):
    rr = lax.broadcasted_iota(jnp.int32, (LANE, LANE), 0) // N_C
    cc = lax.broadcasted_iota(jnp.int32, (LANE, LANE), 1) // N_C
    ones_bd = (rr == cc).astype(BF16)
    hi = x.astype(BF16)
    lo = (x - hi.astype(F32)).astype(BF16)
    outs = []
    for p in range(N_PAIR):
        sl = slice(p * LANE, (p + 1) * LANE)
        outs.append(jnp.dot(hi[:, sl], ones_bd, preferred_element_type=F32)
                    + jnp.dot(lo[:, sl], ones_bd, preferred_element_type=F32))
    return jnp.concatenate(outs, axis=1)


def _rwkv_prep_kernel(cr_ref, ck_ref, cv_ref, wa_ref, pr_ref, pk_ref, pv_ref, pwa_ref,
                      mu_ref, muwa_ref, w0_ref, wup_ref, a0_ref, aup_ref, kk_ref, ka_ref,
                      r_out, k_out, v_out, kkn_out, b_out, g_out, *, T):
    i = pl.program_id(0)
    live = (i > 0).astype(F32)

    def shift_mix(cur_ref, prev_ref, mu):
        cur = cur_ref[...]
        rows = lax.broadcasted_iota(jnp.int32, cur.shape, 0)
        first = jnp.broadcast_to(prev_ref[SUB - 1:SUB, :] * live, cur.shape)
        prev = jnp.where(rows == 0, first, pltpu.roll(cur, 1, 0))
        return cur + (prev - cur) * mu

    r = shift_mix(cr_ref, pr_ref, mu_ref[0:1, :])
    kraw = shift_mix(ck_ref, pk_ref, mu_ref[1:2, :])
    v = shift_mix(cv_ref, pv_ref, mu_ref[2:3, :])
    wa = shift_mix(wa_ref, pwa_ref, muwa_ref[...])
    zw = w0_ref[...] + jnp.dot(jnp.tanh(wa).astype(BF16), wup_ref[...], preferred_element_type=F32)
    za = a0_ref[...] + jnp.dot(wa.astype(BF16), aup_ref[...], preferred_element_type=F32)
    g = (-0.6065306597126334) * _sigmoid(zw)
    a = _sigmoid(za)
    kk = kraw * kk_ref[...]
    ss = _head_sum(kk * kk)
    kkn = kk * lax.rsqrt(jnp.maximum(ss, 1e-24))
    r_out[...] = r
    k_out[...] = kraw * (1.0 + (a - 1.0) * ka_ref[...])
    v_out[...] = v
    kkn_out[...] = kkn
    b_out[...] = kkn * a
    g_out[...] = g


def _rwkv_prep(P, mu3, muwa, w0, wup, a0, aup, k_k, k_a):
    S = P.shape[0]
    T = min(256, S)
    nb = D_BR // LANE
    prev = lambda i: jnp.maximum(i * (T // SUB) - 1, 0)
    row = lambda n: pl.BlockSpec((1, n), lambda i: (0, 0))
    big = pl.BlockSpec((T, D_BR), lambda i: (i, 0))
    return pl.pallas_call(
        functools.partial(_rwkv_prep_kernel, T=T),
        grid=(S // T,),
        in_specs=[pl.BlockSpec((T, D_BR), lambda i: (i, PC_CR // nb)),
                  pl.BlockSpec((T, D_BR), lambda i: (i, PC_CK // nb)),
                  pl.BlockSpec((T, D_BR), lambda i: (i, PC_CV // nb)),
                  pl.BlockSpec((T, LANE), lambda i: (i, PC_WA)),
                  pl.BlockSpec((SUB, D_BR), lambda i: (prev(i), PC_CR // nb)),
                  pl.BlockSpec((SUB, D_BR), lambda i: (prev(i), PC_CK // nb)),
                  pl.BlockSpec((SUB, D_BR), lambda i: (prev(i), PC_CV // nb)),
                  pl.BlockSpec((SUB, LANE), lambda i: (prev(i), PC_WA)),
                  pl.BlockSpec((3, D_BR), lambda i: (0, 0)),
                  row(LANE), row(D_BR),
                  pl.BlockSpec((LANE, D_BR), lambda i: (0, 0)),
                  row(D_BR),
                  pl.BlockSpec((LANE, D_BR), lambda i: (0, 0)),
                  row(D_BR), row(D_BR)],
        out_specs=[big] * 6,
        out_shape=[jax.ShapeDtypeStruct((S, D_BR), F32)] * 6,
        compiler_params=_cparams(("parallel",)),
        name="rwkv_prep",
    )(P, P, P, P, P, P, P, P, mu3, muwa, w0, wup, a0, aup, k_k, k_a)


def _mm(a, b):
    return jnp.dot(a.astype(BF16), b.astype(BF16), preferred_element_type=F32)


def _mm_nt(a, b):
    return lax.dot_general(a.astype(BF16), b.astype(BF16), (((1,), (1,)), ((), ())),
                           preferred_element_type=F32)


def _mm_tn(a, b):
    return lax.dot_general(a.astype(BF16), b.astype(BF16), (((0,), (0,)), ((), ())),
                           preferred_element_type=F32)


def _rwkv_chunk_kernel(r_ref, k_ref, v_ref, kkn_ref, b_ref, g_ref,
                       mt_ref, nt_ref, rh_ref, yl_ref):
    L = CHUNK
    g = g_ref[...]
    rr = lax.broadcasted_iota(jnp.int32, (L, L), 0)
    cc = lax.broadcasted_iota(jnp.int32, (L, L), 1)
    G = jnp.dot((cc <= rr).astype(F32), g, precision=lax.Precision.HIGHEST,
                preferred_element_type=F32)
    Ge = G - g
    Gm = G[L // 2 - 1:L // 2, :]
    GL = G[L - 1:L, :]
    na = -kkn_ref[...]
    bv = b_ref[...]
    kv = k_ref[...]
    rv = r_ref[...]
    al = na * jnp.exp(Ge)
    al_m = na * jnp.exp(Ge - Gm)
    e_b = jnp.exp(Gm - G)
    be_m = bv * e_b
    ka_m = kv * e_b
    rho = rv * jnp.exp(G)
    rho_m = rv * jnp.exp(G - Gm)
    e_l = jnp.exp(GL - G)
    b_hat = bv * e_l
    k_hat = kv * e_l
    d_row = jnp.exp(GL)
    vm = v_ref[...]

    lane1 = lax.broadcasted_iota(jnp.int32, (L, LANE), 1)
    lo1 = lane1 < N_C
    lane2 = lax.broadcasted_iota(jnp.int32, (L, 2 * LANE), 1)
    lo2 = (lane2 & (LANE - 1)) < N_C
    r4 = lax.broadcasted_iota(jnp.int32, (4 * L, L), 0)
    c4 = lax.broadcasted_iota(jnp.int32, (4 * L, L), 1)
    t4 = r4 & (L - 1)
    tri4 = c4 < t4 + jnp.where(r4 < 2 * L, 0, 1)
    rb = lax.broadcasted_iota(jnp.int32, (LANE, LANE), 0)
    cb = lax.broadcasted_iota(jnp.int32, (LANE, LANE), 1)
    bd = (rb // N_C) == (cb // N_C)
    eye = rb == cb

    pairs = range(N_PAIR)
    sls = [slice(p * LANE, (p + 1) * LANE) for p in pairs]
    vms = [vm[:, sl].astype(BF16) for sl in sls]
    a_b, a_k = [], []
    for p in pairs:
        alm_p, rhm_p = al_m[:, sls[p]], rho_m[:, sls[p]]
        lhs4 = jnp.concatenate([jnp.where(lo1, alm_p, 0.0), jnp.where(lo1, 0.0, alm_p),
                                jnp.where(lo1, rhm_p, 0.0), jnp.where(lo1, 0.0, rhm_p)], axis=0).astype(BF16)
        a_b.append(jnp.where(tri4, _mm_nt(lhs4, be_m[:, sls[p]]), 0.0).astype(BF16))
        a_k.append(jnp.where(tri4, _mm_nt(lhs4, ka_m[:, sls[p]]), 0.0).astype(BF16))
    x0 = []
    for p in pairs:
        akv = jnp.where(lo1, _mm(a_k[p][0:L], vms[p]), _mm(a_k[p][L:2 * L], vms[p]))
        x0.append(jnp.concatenate([al[:, sls[p]], akv], axis=1))
    chains = [(p, hh) for p in pairs for hh in range(2)]
    apow = [a_b[p][hh * L:(hh + 1) * L] for p, hh in chains]
    x0b = [x.astype(BF16) for x in x0]
    xs = [x0[p] + _mm(apow[n], x0b[p]) for n, (p, hh) in enumerate(chains)]
    for _ in range(5):
        apow = [_mm(a, a).astype(BF16) for a in apow]
        xs = [x + _mm(a, x) for a, x in zip(apow, xs)]
    pq = [jnp.where(lo2, xs[2 * p], xs[2 * p + 1]).astype(BF16) for p in pairs]
    rbs = [_mm(a_b[p][(2 + hh) * L:(3 + hh) * L], pq[p]) for p, hh in chains]
    rks = [_mm(a_k[p][(2 + hh) * L:(3 + hh) * L], vms[p]) for p, hh in chains]
    g1 = [_mm_tn(b_hat[:, sls[p]], pq[p]) for p in pairs]
    g2 = [_mm_tn(k_hat[:, sls[p]], vms[p]) for p in pairs]
    for p in pairs:
        sl = sls[p]
        r0, r1 = rbs[2 * p], rbs[2 * p + 1]
        rh_ref[:, sl] = (rho[:, sl] + jnp.where(lo1, r0[:, :LANE], r1[:, :LANE])).astype(rh_ref.dtype)
        yl_ref[:, sl] = jnp.where(lo1, r0[:, LANE:] + rks[2 * p], r1[:, LANE:] + rks[2 * p + 1])
        d_p = jnp.broadcast_to(d_row[:, sl], (LANE, LANE))
        mt_ref[0, p] = (jnp.where(eye, d_p, 0.0) + jnp.where(bd, g1[p][:, :LANE], 0.0)).astype(mt_ref.dtype)
        nt_ref[0, p] = jnp.where(bd, g1[p][:, LANE:] + g2[p], 0.0)


def _rwkv_chunks(r, k, v, kkn, b, g):
    S = r.shape[0]
    nc = S // CHUNK
    big = pl.BlockSpec((CHUNK, D_BR), lambda c: (c, 0))
    mat = pl.BlockSpec((1, N_PAIR, LANE, LANE), lambda c: (c, 0, 0, 0))
    return pl.pallas_call(
        _rwkv_chunk_kernel,
        grid=(nc,),
        in_specs=[big] * 6,
        out_specs=[mat, mat, big, big],
        out_shape=[jax.ShapeDtypeStruct((nc, N_PAIR, LANE, LANE), BF16),
                   jax.ShapeDtypeStruct((nc, N_PAIR, LANE, LANE), F32),
                   jax.ShapeDtypeStruct((S, D_BR), BF16),
                   jax.ShapeDtypeStruct((S, D_BR), F32)],
        compiler_params=_cparams(("parallel",)),
        name="rwkv_chunks",
    )(r, k, v, kkn, b, g)


def _rwkv_seq_kernel(mt_ref, nt_ref, rh_ref, yl_ref, r_ref, k_ref, v_ref, gc_ref,
                     rk_ref, lnw_ref, lnb_ref, o_ref, z_ref, y_ref, *, NCH):
    i = pl.program_id(0)

    @pl.when(i == 0)
    def _():
        z_ref[...] = jnp.zeros_like(z_ref)

    for p in range(N_PAIR):
        sl = slice(p * LANE, (p + 1) * LANE)
        z = z_ref[p]
        for c in range(NCH):
            rows = slice(c * CHUNK, (c + 1) * CHUNK)
            zb = z.astype(BF16)
            y_ref[rows, sl] = jnp.dot(rh_ref[rows, sl], zb, preferred_element_type=F32) + yl_ref[rows, sl]
            z = jnp.dot(mt_ref[c, p], zb, preferred_element_type=F32) + nt_ref[c, p]
        z_ref[p] = z

    y = y_ref[...]
    inv_n = 1.0 / N_C
    mu = _head_sum(y) * inv_n
    yc = y - mu
    var = _head_sum(yc * yc) * inv_n
    yn = yc * lax.rsqrt(var + GN_EPS) * lnw_ref[...] + lnb_ref[...]
    vv = v_ref[...]
    bonus = _head_sum(r_ref[...] * k_ref[...] * rk_ref[...]) * vv
    o_ref[...] = ((yn + bonus) * _silu(gc_ref[...])).astype(o_ref.dtype)


def _rwkv_seq(mt, nt, rh, yl, r, k, v, P, r_k, ln_w, ln_b):
    S = r.shape[0]
    T = min(256, S)
    nch = T // CHUNK
    nb = D_BR // LANE
    big = pl.BlockSpec((T, D_BR), lambda i: (i, 0))
    mat = pl.BlockSpec((nch, N_PAIR, LANE, LANE), lambda i: (i, 0, 0, 0))
    row = pl.BlockSpec((1, D_BR), lambda i: (0, 0))
    return pl.pallas_call(
        functools.partial(_rwkv_seq_kernel, NCH=nch),
        grid=(S // T,),
        in_specs=[mat, mat, big, big, big, big, big,
                  pl.BlockSpec((T, D_BR), lambda i: (i, PC_GC // nb)),
                  row, row, row],
        out_specs=big,
        out_shape=jax.ShapeDtypeStruct((S, D_BR), BF16),
        scratch_shapes=[pltpu.VMEM((N_PAIR, LANE, LANE), F32),
                        pltpu.VMEM((T, D_BR), F32)],
        compiler_params=_cparams(("arbitrary",)),
        name="rwkv_seq",
    )(mt, nt, rh, yl, r, k, v, P, r_k, ln_w, ln_b)


def _merge_kernel(h_ref, wg0_ref, wg1_ref, wg2_ref, bm_ref, ya_ref, yb_ref, yc_ref,
                  wb0_ref, wb1_ref, wb2_ref, o_ref):
    h = h_ref[...]
    bm = bm_ref[...]
    acc = None
    for n, (wg_ref, y_ref, wb_ref) in enumerate(((wg0_ref, ya_ref, wb0_ref),
                                                  (wg1_ref, yb_ref, wb1_ref),
                                                  (wg2_ref, yc_ref, wb2_ref))):
        gate = _sigmoid(jnp.dot(h, wg_ref[...], preferred_element_type=F32) + bm[n:n + 1, :])
        term = gate * jnp.dot(y_ref[...], wb_ref[...], preferred_element_type=F32)
        acc = term if acc is None else acc + term
    o_ref[...] = acc.astype(o_ref.dtype)


def _merge(h, w_mg, b_merge, ya, yb, yc, w_br):
    S = h.shape[0]
    tm = min(512, S)
    tn = 512
    nj = D_MODEL // tn
    ysp = pl.BlockSpec((tm, D_BR), lambda i, j: (i, 0))
    return pl.pallas_call(
        _merge_kernel,
        grid=(S // tm, nj),
        in_specs=[pl.BlockSpec((tm, D_MODEL), lambda i, j: (i, 0)),
                  pl.BlockSpec((D_MODEL, tn), lambda i, j: (0, j)),
                  pl.BlockSpec((D_MODEL, tn), lambda i, j: (0, nj + j)),
                  pl.BlockSpec((D_MODEL, tn), lambda i, j: (0, 2 * nj + j)),
                  pl.BlockSpec((3, tn), lambda i, j: (0, j)),
                  ysp, ysp, ysp,
                  pl.BlockSpec((None, D_BR, tn), lambda i, j: (0, 0, j)),
                  pl.BlockSpec((None, D_BR, tn), lambda i, j: (1, 0, j)),
                  pl.BlockSpec((None, D_BR, tn), lambda i, j: (2, 0, j))],
        out_specs=pl.BlockSpec((tm, tn), lambda i, j: (i, j)),
        out_shape=jax.ShapeDtypeStruct((S, D_MODEL), BF16),
        compiler_params=_cparams(("parallel", "parallel")),
        name="merge",
    )(h, w_mg, w_mg, w_mg, b_merge, ya, yb, yc, w_br, w_br, w_br)


def _out_kernel(m_ref, w_ref, x_ref, g_ref, o_ref):
    y = jnp.dot(m_ref[...], w_ref[...], preferred_element_type=F32)
    yn = y * lax.rsqrt(jnp.mean(y * y, axis=-1, keepdims=True) + NORM_EPS)
    o_ref[...] = x_ref[...] + yn * g_ref[...]


def _out_proj(m, w_out, x, g):
    S = m.shape[0]
    tm = min(512, S)
    blk = pl.BlockSpec((tm, D_MODEL), lambda i: (i, 0))
    return pl.pallas_call(
        _out_kernel,
        grid=(S // tm,),
        in_specs=[blk,
                  pl.BlockSpec((D_MODEL, D_MODEL), lambda i: (0, 0)),
                  blk,
                  pl.BlockSpec((1, D_MODEL), lambda i: (0, 0))],
        out_specs=blk,
        out_shape=jax.ShapeDtypeStruct((S, D_MODEL), F32),
        compiler_params=_cparams(("parallel",)),
        name="out_proj",
    )(m, w_out, x, g.reshape(1, D_MODEL))


def _layer_weights(w_in, lru_gate_w, rwkv_w_up, rwkv_a_up):
    o = 0
    xa_w, ga_w = w_in[:, o:o + D_BR], w_in[:, o + D_BR:o + 2 * D_BR]
    o += 2 * D_BR
    qkv_w = w_in[:, o:o + 3 * D_BR]
    fl_w = w_in[:, o + 3 * D_BR:o + 3 * D_BR + H_B]
    gb_w = w_in[:, o + 3 * D_BR + H_B:o + 4 * D_BR + H_B]
    o += 4 * D_BR + H_B
    pc_w = w_in[:, o:o + 3 * D_BR + 2 * LORA]
    o += 3 * D_BR + 2 * LORA
    gc_w = w_in[:, o:o + D_BR]
    o += D_BR
    mg_w = w_in[:, o:]
    pad = jnp.zeros((D_MODEL, P_COLS - (PC_FL * LANE + H_B)), F32)
    w_p = jnp.concatenate([xa_w, ga_w, gb_w, gc_w, pc_w, fl_w, pad], axis=1).astype(BF16)
    eye = jnp.eye(H_A, dtype=F32)
    wg = jnp.einsum('ghij,hk->ghikj', lru_gate_w, eye).reshape(2, D_BR, D_BR)
    wg = jnp.concatenate([wg[0], wg[1]], axis=1).astype(BF16)
    z = jnp.zeros((LORA, D_BR), F32)
    wup = jnp.concatenate([rwkv_w_up, z], axis=0).astype(BF16)
    aup = jnp.concatenate([z, rwkv_a_up], axis=0).astype(BF16)
    return w_p, qkv_w.astype(BF16), mg_w.astype(BF16), wg, wup, aup


def kernel(x, pre_norm_w, post_norm_w, w_in, b_merge, conv_w, conv_b, lru_gate_w, lru_gate_b, lru_lambda,
           fox_b_f, rwkv_mu, rwkv_w0, rwkv_w_up, rwkv_a0, rwkv_a_up, rwkv_k_k, rwkv_k_a, rwkv_r_k,
           rwkv_ln_w, rwkv_ln_b, w_branch, w_out):
    B, S, _ = x.shape
    assert B == 1 and S % CHUNK == 0
    depth = w_in.shape[0]
    xs = x.reshape(S, D_MODEL)
    ones_p = jnp.ones((1, P_COLS), F32)
    qscale = jnp.concatenate([jnp.full((1, D_BR), DH_B ** -0.5 * LOG2E, F32), jnp.ones((1, 2 * D_BR), F32)], axis=1)
    row = lambda a: a.reshape(1, -1)
    for l in range(depth):
        w_p, w_qkv, w_mg, wg, wup, aup = _layer_weights(w_in[l], lru_gate_w[l], rwkv_w_up[l], rwkv_a_up[l])
        h = _rmsnorm_bf16(xs, pre_norm_w[l])
        P = _proj(h, w_p, ones_p, F32, 512)
        qkv = _proj(h, w_qkv, qscale, BF16, 512)
        ya = _rglru(P, conv_w[l], row(conv_b[l]), wg, row(lru_gate_b[l]), row(lru_lambda[l]))
        bf_pad = jnp.concatenate([fox_b_f[l], jnp.zeros((LANE - H_B,), F32)]).reshape(1, LANE)
        kx, cs = _fox_c(P, bf_pad)
        yb = _fox(qkv, kx, cs, P)
        mu = rwkv_mu[l]
        mu3 = mu[:3 * D_BR].reshape(3, D_BR)
        muwa = mu[3 * D_BR:].reshape(1, LANE)
        r, k, v, kkn, b, g = _rwkv_prep(P, mu3, muwa, row(rwkv_w0[l]), wup, row(rwkv_a0[l]), aup,
                                        row(rwkv_k_k[l]), row(rwkv_k_a[l]))
        mt, nt, rh, yl = _rwkv_chunks(r, k, v, kkn, b, g)
        yc = _rwkv_seq(mt, nt, rh, yl, r, k, v, P, row(rwkv_r_k[l]), row(rwkv_ln_w[l]), row(rwkv_ln_b[l]))
        m = _merge(h, w_mg, b_merge[l], ya, yb, yc, w_branch[l].astype(BF16))
        xs = _out_proj(m, w_out[l].astype(BF16), xs, post_norm_w[l])
    return xs.reshape(B, S, D_MODEL)
```

```python
import functools

import jax
import jax.numpy as jnp
from jax import lax
from jax.experimental import pallas as pl
from jax.experimental.pallas import tpu as pltpu

F32 = jnp.float32
BF16 = jnp.bfloat16

D_MODEL = 2048
D_BR = 1024
H_A, BW_A = 16, 64
CONV_W = 4
LRU_C = 8.0
H_B, DH_B = 8, 128
H_C, N_C = 16, 64
LORA = 64
NORM_EPS = 1e-6
GN_EPS = 64e-5
LANE = 128
SUB = 8
CHUNK = 64
N_PAIR = H_C // 2
NEG = -1e30
LOG2E = 1.4426950408889634
FOX_T = 512

PC_XA, PC_GA, PC_GB, PC_GC, PC_CR, PC_CK, PC_CV = 0, 8, 16, 24, 32, 40, 48
PC_WA, PC_FL = 56, 57
P_COLS = 60 * LANE

VMEM_LIMIT = 48 * 1024 * 1024


def _cparams(sem):
    return pltpu.CompilerParams(dimension_semantics=sem, vmem_limit_bytes=VMEM_LIMIT)


def _sigmoid(x):
    return 1.0 / (1.0 + jnp.exp(-x))


def _silu(x):
    return x * _sigmoid(x)


def _softplus(z):
    return jnp.maximum(z, 0.0) + jnp.log1p(jnp.exp(-jnp.abs(z)))


def _norm_kernel(x_ref, g_ref, o_ref):
    x = x_ref[...]
    y = x * lax.rsqrt(jnp.mean(x * x, axis=-1, keepdims=True) + NORM_EPS)
    o_ref[...] = (y * g_ref[...]).astype(o_ref.dtype)


def _rmsnorm_bf16(x, g):
    S = x.shape[0]
    tm = min(512, S)
    return pl.pallas_call(
        _norm_kernel,
        grid=(S // tm,),
        in_specs=[pl.BlockSpec((tm, D_MODEL), lambda i: (i, 0)),
                  pl.BlockSpec((1, D_MODEL), lambda i: (0, 0))],
        out_specs=pl.BlockSpec((tm, D_MODEL), lambda i: (i, 0)),
        out_shape=jax.ShapeDtypeStruct((S, D_MODEL), BF16),
        compiler_params=_cparams(("parallel",)),
        name="rmsnorm",
    )(x, g.reshape(1, D_MODEL))


def _proj_kernel(h_ref, w_ref, s_ref, o_ref):
    acc = jnp.dot(h_ref[...], w_ref[...], preferred_element_type=F32)
    o_ref[...] = (acc * s_ref[...]).astype(o_ref.dtype)


def _proj(h, w, colscale, out_dtype, tn):
    S, K = h.shape
    N = w.shape[1]
    tm = min(1024, S)
    return pl.pallas_call(
        _proj_kernel,
        grid=(S // tm, N // tn),
        in_specs=[pl.BlockSpec((tm, K), lambda i, j: (i, 0)),
                  pl.BlockSpec((K, tn), lambda i, j: (0, j)),
                  pl.BlockSpec((1, tn), lambda i, j: (0, j))],
        out_specs=pl.BlockSpec((tm, tn), lambda i, j: (i, j)),
        out_shape=jax.ShapeDtypeStruct((S, N), out_dtype),
        compiler_params=_cparams(("parallel", "parallel")),
        name="in_proj",
    )(h, w, colscale)


def _rglru_kernel(xa_ref, ga_ref, cw_ref, cb_ref, wg_ref, gbias_ref, lam_ref, o_ref,
                  xe_ref, hc_ref, a_ref, b_ref, *, T):
    i = pl.program_id(0)

    @pl.when(i == 0)
    def _():
        xe_ref[0:SUB, :] = jnp.zeros((SUB, D_BR), F32)
        hc_ref[...] = jnp.zeros_like(hc_ref)

    xe_ref[SUB:SUB + T, :] = xa_ref[...]
    cw = cw_ref[...]
    xc = cb_ref[...] + xe_ref[SUB:SUB + T, :] * cw[3:4]
    for j in range(CONV_W - 1):
        xc = xc + xe_ref[SUB - 3 + j:SUB - 3 + j + T, :] * cw[j:j + 1]
    xe_ref[0:SUB, :] = xe_ref[T:T + SUB, :]

    gates = jnp.dot(xc.astype(BF16), wg_ref[...], preferred_element_type=F32) + gbias_ref[...]
    r = _sigmoid(gates[:, :D_BR])
    ig = _sigmoid(gates[:, D_BR:])
    log_a = (-LRU_C) * r * _softplus(-lam_ref[...])
    a = jnp.exp(log_a)
    mult = jnp.sqrt(-jnp.tanh(log_a) * (1.0 + a * a))
    rows = lax.broadcasted_iota(jnp.int32, (T, D_BR), 0)
    mult = jnp.where((rows + i * T) == 0, 1.0, mult)
    b = mult * ig * xc

    rmod = rows & (SUB - 1)
    for d in (1, 2, 4):
        keep = rmod >= d
        a_s = pltpu.roll(a, d, 0)
        b_s = pltpu.roll(b, d, 0)
        b = jnp.where(keep, a * b_s + b, b)
        a = jnp.where(keep, a * a_s, a)
    a_ref[...] = a
    b_ref[...] = b
    carry = hc_ref[...]
    for g in range(T // SUB):
        sl = slice(g * SUB, (g + 1) * SUB)
        hg = b_ref[sl, :] + a_ref[sl, :] * carry
        b_ref[sl, :] = hg
        carry = jnp.broadcast_to(hg[SUB - 1:SUB, :], (SUB, D_BR))
    hc_ref[...] = carry
    o_ref[...] = (b_ref[...] * _silu(ga_ref[...])).astype(o_ref.dtype)


def _rglru(P, conv_w, conv_b, wg, gbias, lam):
    S = P.shape[0]
    T = min(256, S)
    nb = D_BR // LANE
    return pl.pallas_call(
        functools.partial(_rglru_kernel, T=T),
        grid=(S // T,),
        in_specs=[pl.BlockSpec((T, D_BR), lambda i: (i, PC_XA // nb)),
                  pl.BlockSpec((T, D_BR), lambda i: (i, PC_GA // nb)),
                  pl.BlockSpec((CONV_W, D_BR), lambda i: (0, 0)),
                  pl.BlockSpec((1, D_BR), lambda i: (0, 0)),
                  pl.BlockSpec((D_BR, 2 * D_BR), lambda i: (0, 0)),
                  pl.BlockSpec((1, 2 * D_BR), lambda i: (0, 0)),
                  pl.BlockSpec((1, D_BR), lambda i: (0, 0))],
        out_specs=pl.BlockSpec((T, D_BR), lambda i: (i, 0)),
        out_shape=jax.ShapeDtypeStruct((S, D_BR), BF16),
        scratch_shapes=[pltpu.VMEM((T + SUB, D_BR), F32),
                        pltpu.VMEM((SUB, D_BR), F32),
                        pltpu.VMEM((T, D_BR), F32),
                        pltpu.VMEM((T, D_BR), F32)],
        compiler_params=_cparams(("arbitrary",)),
        name="rglru",
    )(P, P, conv_w, conv_b, wg, gbias, lam)


def _fox_c_kernel(fl_ref, bf_ref, kx_ref, cs_ref, carry_ref, *, T):
    i = pl.program_id(0)

    @pl.when(i == 0)
    def _():
        carry_ref[...] = jnp.zeros_like(carry_ref)

    z = fl_ref[...] + bf_ref[...]
    lf = (jnp.minimum(z, 0.0) - jnp.log1p(jnp.exp(-jnp.abs(z)))) * LOG2E
    rr = lax.broadcasted_iota(jnp.int32, (T, T), 0)
    cc = lax.broadcasted_iota(jnp.int32, (T, T), 1)
    lc = jnp.dot((cc <= rr).astype(F32), lf, precision=lax.Precision.HIGHEST,
                 preferred_element_type=F32)
    carry = carry_ref[...]
    cs_ref[0] = carry
    carry_ref[...] = carry + jnp.broadcast_to(lc[T - 1:T, :], (SUB, LANE))
    x = -lc
    hi = x.astype(BF16)
    r1 = x - hi.astype(F32)
    mid = r1.astype(BF16)
    lo = (r1 - mid.astype(F32)).astype(BF16)
    pieces = jnp.concatenate([hi, mid, lo], axis=1)
    rs = lax.broadcasted_iota(jnp.int32, (3 * LANE, D_BR), 0)
    cs = lax.broadcasted_iota(jnp.int32, (3 * LANE, D_BR), 1)
    sel = (cs == (rs % LANE) * LANE + rs // LANE).astype(BF16)
    kx_ref[...] = jnp.dot(pieces, sel, preferred_element_type=F32).astype(BF16)


def _fox_c(P, bf_pad):
    S = P.shape[0]
    T = min(FOX_T, S // 2)
    return pl.pallas_call(
        functools.partial(_fox_c_kernel, T=T),
        grid=(S // T,),
        in_specs=[pl.BlockSpec((T, LANE), lambda i: (i, PC_FL)),
                  pl.BlockSpec((1, LANE), lambda i: (0, 0))],
        out_specs=[pl.BlockSpec((T, D_BR), lambda i: (i, 0)),
                   pl.BlockSpec((1, SUB, LANE), lambda i: (i, 0, 0))],
        out_shape=[jax.ShapeDtypeStruct((S, D_BR), BF16),
                   jax.ShapeDtypeStruct((S // T, SUB, LANE), F32)],
        scratch_shapes=[pltpu.VMEM((SUB, LANE), F32)],
        compiler_params=_cparams(("arbitrary",)),
        name="fox_cumgate",
    )(P, bf_pad)


def _fox_kernel(cs_ref, q_ref, k_ref, kx_ref, v_ref, gb_ref, o_ref, *, T):
    h = pl.program_id(0)
    i = pl.program_id(1)
    Q = 2 * T
    lane = lax.broadcasted_iota(jnp.int32, (Q, DH_B), 1)
    qa = jnp.concatenate([q_ref[...], jnp.where(lane < 3, 1.0, 0.0).astype(BF16)], axis=1)
    ones_v = jnp.ones((T, DH_B), BF16)
    c_i = cs_ref[h, 2 * i]
    col_row = lax.broadcasted_iota(jnp.int32, (Q, T), 1) - lax.broadcasted_iota(jnp.int32, (Q, T), 0)

    def chunk(j):
        ks = pl.multiple_of(j * T, T)
        kc = jnp.concatenate([k_ref[pl.ds(ks, T), :], kx_ref[pl.ds(ks, T), :]], axis=1)
        vc = jnp.concatenate([v_ref[pl.ds(ks, T), :], ones_v], axis=1)
        return kc, vc, c_i - cs_ref[h, j]

    def qk(kc):
        return lax.dot_general(qa, kc, (((1,), (1,)), ((), ())), preferred_element_type=F32)

    def sm_pv(carry, s, vc, d):
        m, acc = carry
        m_new = jnp.maximum(m, jnp.max(s, axis=-1, keepdims=True) + d)
        p = jnp.exp2(s - (m_new - d))
        acc = jnp.exp2(m - m_new) * acc + jnp.dot(p.astype(BF16), vc, preferred_element_type=F32)
        return m_new, acc

    def chunks(j0, n, carry, diag=False):
        cks = [chunk(j0 + c) for c in range(n)]
        s_next = qk(cks[0][0])
        for c in range(n):
            s = s_next
            if c + 1 < n:
                s_next = qk(cks[c + 1][0])
            if diag:
                s = jnp.where(col_row <= -c * T, s, NEG)
            carry = sm_pv(carry, s, cks[c][1], cks[c][2])
        return carry

    init = (jnp.full((Q, 1), NEG, F32), jnp.zeros((Q, 2 * DH_B), F32))
    carry = lax.fori_loop(0, i // 2, lambda jj, c: chunks(4 * jj, 4, c), init)
    carry = lax.cond(i % 2 == 1, lambda c: chunks(2 * i - 2, 2, c), lambda c: c, carry)
    _, acc = chunks(2 * i, 2, carry, diag=True)
    o_ref[...] = (acc[:, :DH_B] / acc[:, DH_B:] * _silu(gb_ref[...])).astype(o_ref.dtype)


def _fox(qkv, kx, cs, P):
    S = qkv.shape[0]
    T = min(FOX_T, S // 2)
    cst = cs[:, 0, :H_B].T
    kv = lambda col0: pl.BlockSpec((S, DH_B), lambda h, i: (0, col0 + h))
    return pl.pallas_call(
        functools.partial(_fox_kernel, T=T),
        grid=(H_B, S // (2 * T)),
        in_specs=[pl.BlockSpec(memory_space=pltpu.SMEM),
                  pl.BlockSpec((2 * T, DH_B), lambda h, i: (i, h)),
                  kv(H_B), kv(0), kv(2 * H_B),
                  pl.BlockSpec((2 * T, DH_B), lambda h, i: (i, PC_GB + h))],
        out_specs=pl.BlockSpec((2 * T, DH_B), lambda h, i: (i, h)),
        out_shape=jax.ShapeDtypeStruct((S, D_BR), BF16),
        compiler_params=_cparams(("parallel", "parallel")),
        name="fox_attn",
    )(cst, qkv, qkv, kx, qkv, P)


def _head_sum(x):
    rr = lax.broadcasted_iota(jnp.int32, (LANE, LANE), 0) // N_C
    cc = lax.broadcasted_iota(jnp.int32, (LANE, LANE), 1) // N_C
    ones_bd = (rr == cc).astype(BF16)
    hi = x.astype(BF16)
    lo = (x - hi.astype(F32)).astype(BF16)
    outs = []
    for p in range(N_PAIR):
        sl = slice(p * LANE, (p + 1) * LANE)
        outs.append(jnp.dot(hi[:, sl], ones_bd, preferred_element_type=F32)
                    + jnp.dot(lo[:, sl], ones_bd, preferred_element_type=F32))
    return jnp.concatenate(outs, axis=1)


def _rwkv_tokens(cr_ref, ck_ref, cv_ref, wa_ref, pr_ref, pk_ref, pv_ref, pwa_ref,
                 mu_ref, muwa_ref, w0_ref, wup_ref, a0_ref, aup_ref, kk_ref, ka_ref):
    live = (pl.program_id(0) > 0).astype(F32)

    def shift_mix(cur_ref, prev_ref, mu):
        cur = cur_ref[...]
        rows = lax.broadcasted_iota(jnp.int32, cur.shape, 0)
        first = jnp.broadcast_to(prev_ref[SUB - 1:SUB, :] * live, cur.shape)
        prev = jnp.where(rows == 0, first, pltpu.roll(cur, 1, 0))
        return cur + (prev - cur) * mu

    r = shift_mix(cr_ref, pr_ref, mu_ref[0:1, :])
    kraw = shift_mix(ck_ref, pk_ref, mu_ref[1:2, :])
    v = shift_mix(cv_ref, pv_ref, mu_ref[2:3, :])
    wa = shift_mix(wa_ref, pwa_ref, muwa_ref[...])
    zw = w0_ref[...] + jnp.dot(jnp.tanh(wa).astype(BF16), wup_ref[...], preferred_element_type=F32)
    za = a0_ref[...] + jnp.dot(wa.astype(BF16), aup_ref[...], preferred_element_type=F32)
    g = (-0.6065306597126334) * _sigmoid(zw)
    a = _sigmoid(za)
    kk = kraw * kk_ref[...]
    ss = _head_sum(kk * kk)
    kkn = kk * lax.rsqrt(jnp.maximum(ss, 1e-24))
    return r, kraw * (1.0 + (a - 1.0) * ka_ref[...]), v, kkn, kkn * a, g


def _mm(a, b):
    return jnp.dot(a.astype(BF16), b.astype(BF16), preferred_element_type=F32)


def _mm_nt(a, b):
    return lax.dot_general(a.astype(BF16), b.astype(BF16), (((1,), (1,)), ((), ())),
                           preferred_element_type=F32)


def _mm_tn(a, b):
    return lax.dot_general(a.astype(BF16), b.astype(BF16), (((0,), (0,)), ((), ())),
                           preferred_element_type=F32)


def _rwkv_chunk_kernel(*refs):
    rk_ref, mt_ref, nt_ref, rh_ref, yl_ref, bonus_ref = refs[16:]
    L = CHUNK
    rv, kv, vm, kkn, bv, g = _rwkv_tokens(*refs[:16])
    bonus_ref[...] = _head_sum(rv * kv * rk_ref[...]) * vm
    rr = lax.broadcasted_iota(jnp.int32, (L, L), 0)
    cc = lax.broadcasted_iota(jnp.int32, (L, L), 1)
    G = jnp.dot((cc <= rr).astype(F32), g, precision=lax.Precision.HIGHEST,
                preferred_element_type=F32)
    Ge = G - g
    Gm = G[L // 2 - 1:L // 2, :]
    GL = G[L - 1:L, :]
    na = -kkn
    al = na * jnp.exp(Ge)
    al_m = na * jnp.exp(Ge - Gm)
    e_b = jnp.exp(Gm - G)
    be_m = bv * e_b
    ka_m = kv * e_b
    rho = rv * jnp.exp(G)
    rho_m = rv * jnp.exp(G - Gm)
    e_l = jnp.exp(GL - G)
    b_hat = bv * e_l
    k_hat = kv * e_l
    d_row = jnp.exp(GL)

    lane1 = lax.broadcasted_iota(jnp.int32, (L, LANE), 1)
    lo1 = lane1 < N_C
    lane2 = lax.broadcasted_iota(jnp.int32, (L, 2 * LANE), 1)
    lo2 = (lane2 & (LANE - 1)) < N_C
    r4 = lax.broadcasted_iota(jnp.int32, (4 * L, L), 0)
    c4 = lax.broadcasted_iota(jnp.int32, (4 * L, L), 1)
    t4 = r4 & (L - 1)
    tri4 = c4 < t4 + jnp.where(r4 < 2 * L, 0, 1)
    rb = lax.broadcasted_iota(jnp.int32, (LANE, LANE), 0)
    cb = lax.broadcasted_iota(jnp.int32, (LANE, LANE), 1)
    bd = (rb // N_C) == (cb // N_C)
    eye = rb == cb

    pairs = range(N_PAIR)
    sls = [slice(p * LANE, (p + 1) * LANE) for p in pairs]
    vms = [vm[:, sl].astype(BF16) for sl in sls]
    a_b, a_k = [], []
    for p in pairs:
        alm_p, rhm_p = al_m[:, sls[p]], rho_m[:, sls[p]]
        lhs4 = jnp.concatenate([jnp.where(lo1, alm_p, 0.0), jnp.where(lo1, 0.0, alm_p),
                                jnp.where(lo1, rhm_p, 0.0), jnp.where(lo1, 0.0, rhm_p)], axis=0).astype(BF16)
        a_b.append(jnp.where(tri4, _mm_nt(lhs4, be_m[:, sls[p]]), 0.0).astype(BF16))
        a_k.append(jnp.where(tri4, _mm_nt(lhs4, ka_m[:, sls[p]]), 0.0).astype(BF16))
    x0 = []
    for p in pairs:
        akv = jnp.where(lo1, _mm(a_k[p][0:L], vms[p]), _mm(a_k[p][L:2 * L], vms[p]))
        x0.append(jnp.concatenate([al[:, sls[p]], akv], axis=1))
    chains = [(p, hh) for p in pairs for hh in range(2)]
    apow = [a_b[p][hh * L:(hh + 1) * L] for p, hh in chains]
    x0b = [x.astype(BF16) for x in x0]
    xs = [x0[p] + _mm(apow[n], x0b[p]) for n, (p, hh) in enumerate(chains)]
    for _ in range(5):
        apow = [_mm(a, a).astype(BF16) for a in apow]
        xs = [x + _mm(a, x) for a, x in zip(apow, xs)]
    pq = [jnp.where(lo2, xs[2 * p], xs[2 * p + 1]).astype(BF16) for p in pairs]
    rbs = [_mm(a_b[p][(2 + hh) * L:(3 + hh) * L], pq[p]) for p, hh in chains]
    rks = [_mm(a_k[p][(2 + hh) * L:(3 + hh) * L], vms[p]) for p, hh in chains]
    g1 = [_mm_tn(b_hat[:, sls[p]], pq[p]) for p in pairs]
    g2 = [_mm_tn(k_hat[:, sls[p]], vms[p]) for p in pairs]
    for p in pairs:
        sl = sls[p]
        r0, r1 = rbs[2 * p], rbs[2 * p + 1]
        rh_ref[:, sl] = (rho[:, sl] + jnp.where(lo1, r0[:, :LANE], r1[:, :LANE])).astype(rh_ref.dtype)
        yl_ref[:, sl] = jnp.where(lo1, r0[:, LANE:] + rks[2 * p], r1[:, LANE:] + rks[2 * p + 1])
        d_p = jnp.broadcast_to(d_row[:, sl], (LANE, LANE))
        mt_ref[0, p] = (jnp.where(eye, d_p, 0.0) + jnp.where(bd, g1[p][:, :LANE], 0.0)).astype(mt_ref.dtype)
        nt_ref[0, p] = jnp.where(bd, g1[p][:, LANE:] + g2[p], 0.0)


def _rwkv_chunks(P, mu3, muwa, w0, wup, a0, aup, k_k, k_a, r_k):
    S = P.shape[0]
    nc = S // CHUNK
    nb = D_BR // LANE
    prev = lambda c: jnp.maximum(c * (CHUNK // SUB) - 1, 0)
    row = lambda n: pl.BlockSpec((1, n), lambda c: (0, 0))
    big = pl.BlockSpec((CHUNK, D_BR), lambda c: (c, 0))
    mat = pl.BlockSpec((1, N_PAIR, LANE, LANE), lambda c: (c, 0, 0, 0))
    return pl.pallas_call(
        _rwkv_chunk_kernel,
        grid=(nc,),
        in_specs=[pl.BlockSpec((CHUNK, D_BR), lambda c: (c, PC_CR // nb)),
                  pl.BlockSpec((CHUNK, D_BR), lambda c: (c, PC_CK // nb)),
                  pl.BlockSpec((CHUNK, D_BR), lambda c: (c, PC_CV // nb)),
                  pl.BlockSpec((CHUNK, LANE), lambda c: (c, PC_WA)),
                  pl.BlockSpec((SUB, D_BR), lambda c: (prev(c), PC_CR // nb)),
                  pl.BlockSpec((SUB, D_BR), lambda c: (prev(c), PC_CK // nb)),
                  pl.BlockSpec((SUB, D_BR), lambda c: (prev(c), PC_CV // nb)),
                  pl.BlockSpec((SUB, LANE), lambda c: (prev(c), PC_WA)),
                  pl.BlockSpec((3, D_BR), lambda c: (0, 0)),
                  row(LANE), row(D_BR),
                  pl.BlockSpec((LANE, D_BR), lambda c: (0, 0)),
                  row(D_BR),
                  pl.BlockSpec((LANE, D_BR), lambda c: (0, 0)),
                  row(D_BR), row(D_BR), row(D_BR)],
        out_specs=[mat, mat, big, big, big],
        out_shape=[jax.ShapeDtypeStruct((nc, N_PAIR, LANE, LANE), BF16),
                   jax.ShapeDtypeStruct((nc, N_PAIR, LANE, LANE), F32),
                   jax.ShapeDtypeStruct((S, D_BR), BF16),
                   jax.ShapeDtypeStruct((S, D_BR), F32),
                   jax.ShapeDtypeStruct((S, D_BR), F32)],
        compiler_params=_cparams(("parallel",)),
        name="rwkv_chunks",
    )(P, P, P, P, P, P, P, P, mu3, muwa, w0, wup, a0, aup, k_k, k_a, r_k)


def _rwkv_seq_kernel(mt_ref, nt_ref, rh_ref, yl_ref, bonus_ref, gc_ref,
                     lnw_ref, lnb_ref, o_ref, z_ref, y_ref, *, NCH):
    i = pl.program_id(0)

    @pl.when(i == 0)
    def _():
        z_ref[...] = jnp.zeros_like(z_ref)

    zs = [z_ref[p] for p in range(N_PAIR)]
    for c in range(NCH):
        rows = slice(c * CHUNK, (c + 1) * CHUNK)
        zb = [z.astype(BF16) for z in zs]
        zs = [jnp.dot(mt_ref[c, p], zb[p], preferred_element_type=F32) + nt_ref[c, p]
              for p in range(N_PAIR)]
        for p in range(N_PAIR):
            sl = slice(p * LANE, (p + 1) * LANE)
            y_ref[rows, sl] = jnp.dot(rh_ref[rows, sl], zb[p], preferred_element_type=F32) + yl_ref[rows, sl]
    for p in range(N_PAIR):
        z_ref[p] = zs[p]

    y = y_ref[...]
    inv_n = 1.0 / N_C
    mu = _head_sum(y) * inv_n
    yc = y - mu
    var = _head_sum(yc * yc) * inv_n
    yn = yc * lax.rsqrt(var + GN_EPS) * lnw_ref[...] + lnb_ref[...]
    o_ref[...] = ((yn + bonus_ref[...]) * _silu(gc_ref[...])).astype(o_ref.dtype)


def _rwkv_seq(mt, nt, rh, yl, bonus, P, ln_w, ln_b):
    S = rh.shape[0]
    T = min(256, S)
    nch = T // CHUNK
    nb = D_BR // LANE
    big = pl.BlockSpec((T, D_BR), lambda i: (i, 0))
    mat = pl.BlockSpec((nch, N_PAIR, LANE, LANE), lambda i: (i, 0, 0, 0))
    row = pl.BlockSpec((1, D_BR), lambda i: (0, 0))
    return pl.pallas_call(
        functools.partial(_rwkv_seq_kernel, NCH=nch),
        grid=(S // T,),
        in_specs=[mat, mat, big, big, big,
                  pl.BlockSpec((T, D_BR), lambda i: (i, PC_GC // nb)),
                  row, row],
        out_specs=big,
        out_shape=jax.ShapeDtypeStruct((S, D_BR), BF16),
        scratch_shapes=[pltpu.VMEM((N_PAIR, LANE, LANE), F32),
                        pltpu.VMEM((T, D_BR), F32)],
        compiler_params=_cparams(("arbitrary",)),
        name="rwkv_seq",
    )(mt, nt, rh, yl, bonus, P, ln_w, ln_b)


def _merge_kernel(h_ref, wg0_ref, wg1_ref, wg2_ref, bm_ref, ya_ref, yb_ref, yc_ref,
                  wb0_ref, wb1_ref, wb2_ref, o_ref):
    h = h_ref[...]
    bm = bm_ref[...]
    acc = None
    for n, (wg_ref, y_ref, wb_ref) in enumerate(((wg0_ref, ya_ref, wb0_ref),
                                                  (wg1_ref, yb_ref, wb1_ref),
                                                  (wg2_ref, yc_ref, wb2_ref))):
        gate = _sigmoid(jnp.dot(h, wg_ref[...], preferred_element_type=F32) + bm[n:n + 1, :])
        term = gate * jnp.dot(y_ref[...], wb_ref[...], preferred_element_type=F32)
        acc = term if acc is None else acc + term
    o_ref[...] = acc.astype(o_ref.dtype)


def _merge(h, w_mg, b_merge, ya, yb, yc, w_br):
    S = h.shape[0]
    tm = min(512, S)
    tn = 512
    nj = D_MODEL // tn
    ysp = pl.BlockSpec((tm, D_BR), lambda i, j: (i, 0))
    return pl.pallas_call(
        _merge_kernel,
        grid=(S // tm, nj),
        in_specs=[pl.BlockSpec((tm, D_MODEL), lambda i, j: (i, 0)),
                  pl.BlockSpec((D_MODEL, tn), lambda i, j: (0, j)),
                  pl.BlockSpec((D_MODEL, tn), lambda i, j: (0, nj + j)),
                  pl.BlockSpec((D_MODEL, tn), lambda i, j: (0, 2 * nj + j)),
                  pl.BlockSpec((3, tn), lambda i, j: (0, j)),
                  ysp, ysp, ysp,
                  pl.BlockSpec((None, D_BR, tn), lambda i, j: (0, 0, j)),
                  pl.BlockSpec((None, D_BR, tn), lambda i, j: (1, 0, j)),
                  pl.BlockSpec((None, D_BR, tn), lambda i, j: (2, 0, j))],
        out_specs=pl.BlockSpec((tm, tn), lambda i, j: (i, j)),
        out_shape=jax.ShapeDtypeStruct((S, D_MODEL), BF16),
        compiler_params=_cparams(("parallel", "parallel")),
        name="merge",
    )(h, w_mg, w_mg, w_mg, b_merge, ya, yb, yc, w_br, w_br, w_br)


def _out_kernel(m_ref, w_ref, x_ref, g_ref, o_ref):
    y = jnp.dot(m_ref[...], w_ref[...], preferred_element_type=F32)
    yn = y * lax.rsqrt(jnp.mean(y * y, axis=-1, keepdims=True) + NORM_EPS)
    o_ref[...] = x_ref[...] + yn * g_ref[...]


def _out_proj(m, w_out, x, g):
    S = m.shape[0]
    tm = min(512, S)
    blk = pl.BlockSpec((tm, D_MODEL), lambda i: (i, 0))
    return pl.pallas_call(
        _out_kernel,
        grid=(S // tm,),
        in_specs=[blk,
                  pl.BlockSpec((D_MODEL, D_MODEL), lambda i: (0, 0)),
                  blk,
                  pl.BlockSpec((1, D_MODEL), lambda i: (0, 0))],
        out_specs=blk,
        out_shape=jax.ShapeDtypeStruct((S, D_MODEL), F32),
        compiler_params=_cparams(("parallel",)),
        name="out_proj",
    )(m, w_out, x, g.reshape(1, D_MODEL))


def _layer_weights(w_in, lru_gate_w, rwkv_w_up, rwkv_a_up):
    o = 0
    xa_w, ga_w = w_in[:, o:o + D_BR], w_in[:, o + D_BR:o + 2 * D_BR]
    o += 2 * D_BR
    qkv_w = w_in[:, o:o + 3 * D_BR]
    fl_w = w_in[:, o + 3 * D_BR:o + 3 * D_BR + H_B]
    gb_w = w_in[:, o + 3 * D_BR + H_B:o + 4 * D_BR + H_B]
    o += 4 * D_BR + H_B
    pc_w = w_in[:, o:o + 3 * D_BR + 2 * LORA]
    o += 3 * D_BR + 2 * LORA
    gc_w = w_in[:, o:o + D_BR]
    o += D_BR
    mg_w = w_in[:, o:]
    pad = jnp.zeros((D_MODEL, P_COLS - (PC_FL * LANE + H_B)), F32)
    w_p = jnp.concatenate([xa_w, ga_w, gb_w, gc_w, pc_w, fl_w, pad], axis=1).astype(BF16)
    eye = jnp.eye(H_A, dtype=F32)
    wg = jnp.einsum('ghij,hk->ghikj', lru_gate_w, eye).reshape(2, D_BR, D_BR)
    wg = jnp.concatenate([wg[0], wg[1]], axis=1).astype(BF16)
    z = jnp.zeros((LORA, D_BR), F32)
    wup = jnp.concatenate([rwkv_w_up, z], axis=0).astype(BF16)
    aup = jnp.concatenate([z, rwkv_a_up], axis=0).astype(BF16)
    return w_p, qkv_w.astype(BF16), mg_w.astype(BF16), wg, wup, aup


def kernel(x, pre_norm_w, post_norm_w, w_in, b_merge, conv_w, conv_b, lru_gate_w, lru_gate_b, lru_lambda,
           fox_b_f, rwkv_mu, rwkv_w0, rwkv_w_up, rwkv_a0, rwkv_a_up, rwkv_k_k, rwkv_k_a, rwkv_r_k,
           rwkv_ln_w, rwkv_ln_b, w_branch, w_out):
    B, S, _ = x.shape
    assert B == 1 and S % CHUNK == 0
    depth = w_in.shape[0]
    xs = x.reshape(S, D_MODEL)
    ones_p = jnp.ones((1, P_COLS), F32)
    qscale = jnp.concatenate([jnp.full((1, D_BR), DH_B ** -0.5 * LOG2E, F32), jnp.ones((1, 2 * D_BR), F32)], axis=1)
    row = lambda a: a.reshape(1, -1)
    for l in range(depth):
        w_p, w_qkv, w_mg, wg, wup, aup = _layer_weights(w_in[l], lru_gate_w[l], rwkv_w_up[l], rwkv_a_up[l])
        h = _rmsnorm_bf16(xs, pre_norm_w[l])
        P = _proj(h, w_p, ones_p, F32, 512)
        qkv = _proj(h, w_qkv, qscale, BF16, 512)
        ya = _rglru(P, conv_w[l], row(conv_b[l]), wg, row(lru_gate_b[l]), row(lru_lambda[l]))
        bf_pad = jnp.concatenate([fox_b_f[l], jnp.zeros((LANE - H_B,), F32)]).reshape(1, LANE)
        kx, cs = _fox_c(P, bf_pad)
        yb = _fox(qkv, kx, cs, P)
        mu = rwkv_mu[l]
        mu3 = mu[:3 * D_BR].reshape(3, D_BR)
        muwa = mu[3 * D_BR:].reshape(1, LANE)
        mt, nt, rh, yl, bonus = _rwkv_chunks(P, mu3, muwa, row(rwkv_w0[l]), wup, row(rwkv_a0[l]), aup,
                                             row(rwkv_k_k[l]), row(rwkv_k_a[l]), row(rwkv_r_k[l]))
        yc = _rwkv_seq(mt, nt, rh, yl, bonus, P, row(rwkv_ln_w[l]), row(rwkv_ln_b[l]))
        m = _merge(h, w_mg, b_merge[l], ya, yb, yc, w_branch[l].astype(BF16))
        xs = _out_proj(m, w_out[l].astype(BF16), xs, post_norm_w[l])
    return xs.reshape(B, S, D_MODEL)
```

```python
import functools

import jax
import jax.numpy as jnp
from jax import lax
from jax.experimental import pallas as pl
from jax.experimental.pallas import tpu as pltpu

F32 = jnp.float32
BF16 = jnp.bfloat16

D_MODEL = 2048
D_BR = 1024
H_A, BW_A = 16, 64
CONV_W = 4
LRU_C = 8.0
H_B, DH_B = 8, 128
H_C, N_C = 16, 64
LORA = 64
NORM_EPS = 1e-6
GN_EPS = 64e-5
LANE = 128
SUB = 8
CHUNK = 64
N_PAIR = H_C // 2
NEG = -1e30
LOG2E = 1.4426950408889634
FOX_Q = 1024
FOX_T = 1024
FOX_NB = 2
FOX_VMEM = 56 * 1024 * 1024

PC_XA, PC_GA, PC_GB, PC_GC, PC_CR, PC_CK, PC_CV = 0, 8, 16, 24, 32, 40, 48
PC_WA, PC_FL = 56, 57
P_COLS = 60 * LANE

VMEM_LIMIT = 48 * 1024 * 1024


def _cparams(sem):
    return pltpu.CompilerParams(dimension_semantics=sem, vmem_limit_bytes=VMEM_LIMIT)


def _sigmoid(x):
    return 1.0 / (1.0 + jnp.exp(-x))


def _silu(x):
    return x * _sigmoid(x)


def _softplus(z):
    return jnp.maximum(z, 0.0) + jnp.log1p(jnp.exp(-jnp.abs(z)))


def _split3(x):
    hi = x.astype(BF16)
    r1 = x - hi.astype(F32)
    mid = r1.astype(BF16)
    return hi, mid, (r1 - mid.astype(F32)).astype(BF16)


def _tri_cumsum(lower_mask, x):
    tri = lower_mask.astype(BF16)
    hi, mid, lo = _split3(x)
    return (jnp.dot(tri, hi, preferred_element_type=F32) + jnp.dot(tri, mid, preferred_element_type=F32)
            + jnp.dot(tri, lo, preferred_element_type=F32))


def _norm_kernel(x_ref, g_ref, o_ref):
    x = x_ref[...]
    y = x * lax.rsqrt(jnp.mean(x * x, axis=-1, keepdims=True) + NORM_EPS)
    o_ref[...] = (y * g_ref[...]).astype(o_ref.dtype)


def _rmsnorm_bf16(x, g):
    S = x.shape[0]
    tm = min(512, S)
    return pl.pallas_call(
        _norm_kernel,
        grid=(S // tm,),
        in_specs=[pl.BlockSpec((tm, D_MODEL), lambda i: (i, 0)),
                  pl.BlockSpec((1, D_MODEL), lambda i: (0, 0))],
        out_specs=pl.BlockSpec((tm, D_MODEL), lambda i: (i, 0)),
        out_shape=jax.ShapeDtypeStruct((S, D_MODEL), BF16),
        compiler_params=_cparams(("parallel",)),
        name="rmsnorm",
    )(x, g.reshape(1, D_MODEL))


def _proj_kernel(h_ref, w_ref, s_ref, o_ref):
    acc = jnp.dot(h_ref[...], w_ref[...], preferred_element_type=F32)
    o_ref[...] = (acc * s_ref[...]).astype(o_ref.dtype)


def _proj(h, w, colscale, out_dtype, tn):
    S, K = h.shape
    N = w.shape[1]
    tm = min(1024, S)
    return pl.pallas_call(
        _proj_kernel,
        grid=(S // tm, N // tn),
        in_specs=[pl.BlockSpec((tm, K), lambda i, j: (i, 0)),
                  pl.BlockSpec((K, tn), lambda i, j: (0, j)),
                  pl.BlockSpec((1, tn), lambda i, j: (0, j))],
        out_specs=pl.BlockSpec((tm, tn), lambda i, j: (i, j)),
        out_shape=jax.ShapeDtypeStruct((S, N), out_dtype),
        compiler_params=_cparams(("parallel", "parallel")),
        name="in_proj",
    )(h, w, colscale)


def _rglru_kernel(xa_ref, ga_ref, cw_ref, cb_ref, wg_ref, gbias_ref, lam_ref, o_ref,
                  xe_ref, hc_ref, a_ref, b_ref, *, T):
    i = pl.program_id(0)

    @pl.when(i == 0)
    def _():
        xe_ref[0:SUB, :] = jnp.zeros((SUB, D_BR), F32)
        hc_ref[...] = jnp.zeros_like(hc_ref)

    xe_ref[SUB:SUB + T, :] = xa_ref[...]
    cw = cw_ref[...]
    xc = cb_ref[...] + xe_ref[SUB:SUB + T, :] * cw[3:4]
    for j in range(CONV_W - 1):
        xc = xc + xe_ref[SUB - 3 + j:SUB - 3 + j + T, :] * cw[j:j + 1]
    xe_ref[0:SUB, :] = xe_ref[T:T + SUB, :]

    gates = jnp.dot(xc.astype(BF16), wg_ref[...], preferred_element_type=F32) + gbias_ref[...]
    r = _sigmoid(gates[:, :D_BR])
    ig = _sigmoid(gates[:, D_BR:])
    log_a = (-LRU_C) * r * _softplus(-lam_ref[...])
    a = jnp.exp(log_a)
    mult = jnp.sqrt(-jnp.tanh(log_a) * (1.0 + a * a))
    rows = lax.broadcasted_iota(jnp.int32, (T, D_BR), 0)
    mult = jnp.where((rows + i * T) == 0, 1.0, mult)
    b = mult * ig * xc

    rmod = rows & (SUB - 1)
    for d in (1, 2, 4):
        keep = rmod >= d
        a_s = pltpu.roll(a, d, 0)
        b_s = pltpu.roll(b, d, 0)
        b = jnp.where(keep, a * b_s + b, b)
        a = jnp.where(keep, a * a_s, a)
    a_ref[...] = a
    b_ref[...] = b
    carry = hc_ref[...]
    for g in range(T // SUB):
        sl = slice(g * SUB, (g + 1) * SUB)
        hg = b_ref[sl, :] + a_ref[sl, :] * carry
        b_ref[sl, :] = hg
        carry = jnp.broadcast_to(hg[SUB - 1:SUB, :], (SUB, D_BR))
    hc_ref[...] = carry
    o_ref[...] = (b_ref[...] * _silu(ga_ref[...])).astype(o_ref.dtype)


def _rglru(P, conv_w, conv_b, wg, gbias, lam):
    S = P.shape[0]
    T = min(256, S)
    nb = D_BR // LANE
    return pl.pallas_call(
        functools.partial(_rglru_kernel, T=T),
        grid=(S // T,),
        in_specs=[pl.BlockSpec((T, D_BR), lambda i: (i, PC_XA // nb)),
                  pl.BlockSpec((T, D_BR), lambda i: (i, PC_GA // nb)),
                  pl.BlockSpec((CONV_W, D_BR), lambda i: (0, 0)),
                  pl.BlockSpec((1, D_BR), lambda i: (0, 0)),
                  pl.BlockSpec((D_BR, 2 * D_BR), lambda i: (0, 0)),
                  pl.BlockSpec((1, 2 * D_BR), lambda i: (0, 0)),
                  pl.BlockSpec((1, D_BR), lambda i: (0, 0))],
        out_specs=pl.BlockSpec((T, D_BR), lambda i: (i, 0)),
        out_shape=jax.ShapeDtypeStruct((S, D_BR), BF16),
        scratch_shapes=[pltpu.VMEM((T + SUB, D_BR), F32),
                        pltpu.VMEM((SUB, D_BR), F32),
                        pltpu.VMEM((T, D_BR), F32),
                        pltpu.VMEM((T, D_BR), F32)],
        compiler_params=_cparams(("arbitrary",)),
        name="rglru",
    )(P, P, conv_w, conv_b, wg, gbias, lam)


def _fox_c_kernel(fl_ref, bf_ref, kx_ref, cs_ref, carry_ref, *, T):
    i = pl.program_id(0)

    @pl.when(i == 0)
    def _():
        carry_ref[...] = jnp.zeros_like(carry_ref)

    z = fl_ref[...] + bf_ref[...]
    lf = (jnp.minimum(z, 0.0) - jnp.log1p(jnp.exp(-jnp.abs(z)))) * LOG2E
    rr = lax.broadcasted_iota(jnp.int32, (T, T), 0)
    cc = lax.broadcasted_iota(jnp.int32, (T, T), 1)
    lc = _tri_cumsum(cc <= rr, lf)
    carry = carry_ref[...]
    cs_ref[0] = carry
    carry_ref[...] = carry + jnp.broadcast_to(lc[T - 1:T, :], (SUB, LANE))
    pieces = jnp.concatenate(_split3(-lc), axis=1)
    rs = lax.broadcasted_iota(jnp.int32, (3 * LANE, D_BR), 0)
    cs = lax.broadcasted_iota(jnp.int32, (3 * LANE, D_BR), 1)
    sel = (cs == (rs % LANE) * LANE + rs // LANE).astype(BF16)
    kx_ref[...] = jnp.dot(pieces, sel, preferred_element_type=F32).astype(BF16)


def _fox_c(P, bf_pad):
    S = P.shape[0]
    T = min(FOX_T, FOX_Q, S // 2)
    return pl.pallas_call(
        functools.partial(_fox_c_kernel, T=T),
        grid=(S // T,),
        in_specs=[pl.BlockSpec((T, LANE), lambda i: (i, PC_FL)),
                  pl.BlockSpec((1, LANE), lambda i: (0, 0))],
        out_specs=[pl.BlockSpec((T, D_BR), lambda i: (i, 0)),
                   pl.BlockSpec((1, SUB, LANE), lambda i: (i, 0, 0))],
        out_shape=[jax.ShapeDtypeStruct((S, D_BR), BF16),
                   jax.ShapeDtypeStruct((S // T, SUB, LANE), F32)],
        scratch_shapes=[pltpu.VMEM((SUB, LANE), F32)],
        compiler_params=_cparams(("arbitrary",)),
        name="fox_cumgate",
    )(P, bf_pad)


def _fox_kernel(cs_ref, q_ref, k_ref, kx_ref, v_ref, gb_ref, o_ref, sh_ref, dp_ref, acc_ref, *, Q, T, NB):
    h = pl.program_id(0)
    i = pl.program_id(1)
    assert Q == T and NB == 2
    lane = lax.broadcasted_iota(jnp.int32, (Q, DH_B), 1)
    qa = jnp.concatenate([q_ref[...], jnp.where(lane < 3, 1.0, 0.0).astype(BF16)], axis=1)
    ones_v = jnp.ones((T, DH_B), BF16)
    c_i = cs_ref[h, i]

    def chunk(j):
        ks = pl.multiple_of(j * T, T)
        kc = jnp.concatenate([k_ref[pl.ds(ks, T), :], kx_ref[pl.ds(ks, T), :]], axis=1)
        vc = jnp.concatenate([v_ref[pl.ds(ks, T), :], ones_v], axis=1)
        return kc, vc, c_i - cs_ref[h, j]

    def qk(kc):
        return lax.dot_general(qa, kc, (((1,), (1,)), ((), ())), preferred_element_type=F32)

    def sm_pv(carry, s, vc, d):
        sh, d_prev, acc = carry
        a = sh + (d_prev - d)
        sh_new = jnp.maximum(a, jnp.max(s, axis=-1, keepdims=True))
        p = jnp.exp2(s - sh_new)
        acc = jnp.exp2(a - sh_new) * acc + jnp.dot(p.astype(BF16), vc, preferred_element_type=F32)
        return sh_new, d, acc

    def chunks(j0, n, carry, diag=False):
        cks = [chunk(j0 + c) for c in range(n)]
        s_next = qk(cks[0][0])
        for c in range(n):
            s = s_next
            if c + 1 < n:
                s_next = qk(cks[c + 1][0])
            elif diag:
                rows = lax.broadcasted_iota(jnp.int32, (Q, T), 0)
                cols = lax.broadcasted_iota(jnp.int32, (Q, T), 1)
                s = jnp.where(cols <= rows, s, NEG)
            carry = sm_pv(carry, s, cks[c][1], cks[c][2])
        return carry

    carry = (jnp.full((Q, 1), NEG, F32), jnp.float32(0.0), jnp.zeros((Q, 2 * DH_B), F32))
    n = i + 1
    tail = jnp.where(n % 2 == 0, 2, jnp.where(n == 1, 1, 3))
    sh, d_prev, acc = lax.fori_loop(0, (n - tail) // 2, lambda jj, c: chunks(2 * jj, 2, c), carry)
    sh_ref[...] = sh
    dp_ref[0] = d_prev
    acc_ref[...] = acc
    for t in (1, 2, 3):
        @pl.when(tail == t)
        def _():
            acc_ref[...] = chunks(n - t, t, (sh_ref[...], dp_ref[0], acc_ref[...]), diag=True)[2]
    acc = acc_ref[...]
    o_ref[...] = (acc[:, :DH_B] / acc[:, DH_B:] * _silu(gb_ref[...])).astype(o_ref.dtype)


def _fox(qkv, kx, cs, P):
    S = qkv.shape[0]
    Q = min(FOX_Q, S // 2)
    T = min(FOX_T, Q)
    cst = cs[:, 0, :H_B].T
    kv = lambda col0: pl.BlockSpec((S, DH_B), lambda h, i: (0, col0 + h))
    return pl.pallas_call(
        functools.partial(_fox_kernel, Q=Q, T=T, NB=FOX_NB),
        grid=(H_B, S // Q),
        in_specs=[pl.BlockSpec(memory_space=pltpu.SMEM),
                  pl.BlockSpec((Q, DH_B), lambda h, i: (i, h)),
                  kv(H_B), kv(0), kv(2 * H_B),
                  pl.BlockSpec((Q, DH_B), lambda h, i: (i, PC_GB + h))],
        out_specs=pl.BlockSpec((Q, DH_B), lambda h, i: (i, h)),
        out_shape=jax.ShapeDtypeStruct((S, D_BR), BF16),
        scratch_shapes=[pltpu.VMEM((Q, 1), F32), pltpu.SMEM((1,), F32),
                        pltpu.VMEM((Q, 2 * DH_B), F32)],
        compiler_params=pltpu.CompilerParams(dimension_semantics=("parallel", "parallel"),
                                             vmem_limit_bytes=FOX_VMEM),
        name="fox_attn",
    )(cst, qkv, qkv, kx, qkv, P)


def _head_sum(x):
    rr = lax.broadcasted_iota(jnp.int32, (LANE, LANE), 0) // N_C
    cc = lax.broadcasted_iota(jnp.int32, (LANE, LANE), 1) // N_C
    ones_bd = (rr == cc).astype(BF16)
    hi = x.astype(BF16)
    lo = (x - hi.astype(F32)).astype(BF16)
    outs = []
    for p in range(N_PAIR):
        sl = slice(p * LANE, (p + 1) * LANE)
        outs.append(jnp.dot(hi[:, sl], ones_bd, preferred_element_type=F32)
                    + jnp.dot(lo[:, sl], ones_bd, preferred_element_type=F32))
    return jnp.concatenate(outs, axis=1)


def _rwkv_tokens(cr_ref, ck_ref, cv_ref, wa_ref, pr_ref, pk_ref, pv_ref, pwa_ref,
                 mu_ref, muwa_ref, w0_ref, wup_ref, a0_ref, aup_ref, kk_ref, ka_ref):
    live = (pl.program_id(0) > 0).astype(F32)

    def shift_mix(cur_ref, prev_ref, mu):
        cur = cur_ref[...]
        rows = lax.broadcasted_iota(jnp.int32, cur.shape, 0)
        first = jnp.broadcast_to(prev_ref[SUB - 1:SUB, :] * live, cur.shape)
        prev = jnp.where(rows == 0, first, pltpu.roll(cur, 1, 0))
        return cur + (prev - cur) * mu

    r = shift_mix(cr_ref, pr_ref, mu_ref[0:1, :])
    kraw = shift_mix(ck_ref, pk_ref, mu_ref[1:2, :])
    v = shift_mix(cv_ref, pv_ref, mu_ref[2:3, :])
    wa = shift_mix(wa_ref, pwa_ref, muwa_ref[...])
    zw = w0_ref[...] + jnp.dot(jnp.tanh(wa).astype(BF16), wup_ref[...], preferred_element_type=F32)
    za = a0_ref[...] + jnp.dot(wa.astype(BF16), aup_ref[...], preferred_element_type=F32)
    g = (-0.6065306597126334) * _sigmoid(zw)
    a = _sigmoid(za)
    kk = kraw * kk_ref[...]
    ss = _head_sum(kk * kk)
    kkn = kk * lax.rsqrt(jnp.maximum(ss, 1e-24))
    return r, kraw * (1.0 + (a - 1.0) * ka_ref[...]), v, kkn, kkn * a, g


def _mm(a, b):
    return jnp.dot(a.astype(BF16), b.astype(BF16), preferred_element_type=F32)


def _mm_nt(a, b):
    return lax.dot_general(a.astype(BF16), b.astype(BF16), (((1,), (1,)), ((), ())),
                           preferred_element_type=F32)


def _mm_tn(a, b):
    return lax.dot_general(a.astype(BF16), b.astype(BF16), (((0,), (0,)), ((), ())),
                           preferred_element_type=F32)


def _rwkv_chunk_kernel(*refs):
    rk_ref, mt_ref, nt_ref, rh_ref, yl_ref, bonus_ref = refs[16:]
    L = CHUNK
    rv, kv, vm, kkn, bv, g = _rwkv_tokens(*refs[:16])
    bonus_ref[...] = _head_sum(rv * kv * rk_ref[...]) * vm
    rr = lax.broadcasted_iota(jnp.int32, (L, L), 0)
    cc = lax.broadcasted_iota(jnp.int32, (L, L), 1)
    G = _tri_cumsum(cc <= rr, g)
    Ge = G - g
    Gm = G[L // 2 - 1:L // 2, :]
    GL = G[L - 1:L, :]
    na = -kkn
    al = na * jnp.exp(Ge)
    al_m = na * jnp.exp(Ge - Gm)
    e_b = jnp.exp(Gm - G)
    be_m = bv * e_b
    ka_m = kv * e_b
    rho = rv * jnp.exp(G)
    rho_m = rv * jnp.exp(G - Gm)
    e_l = jnp.exp(GL - G)
    b_hat = bv * e_l
    k_hat = kv * e_l
    d_row = jnp.exp(GL)

    lo1 = lax.broadcasted_iota(jnp.int32, (L, LANE), 1) < N_C
    r4 = lax.broadcasted_iota(jnp.int32, (4 * L, LANE), 0)
    c4 = lax.broadcasted_iota(jnp.int32, (4 * L, LANE), 1) & (N_C - 1)
    tri4 = c4 < (r4 & (L - 1)) + jnp.where(r4 < 2 * L, 0, 1)
    rb = lax.broadcasted_iota(jnp.int32, (LANE, LANE), 0)
    cb = lax.broadcasted_iota(jnp.int32, (LANE, LANE), 1)
    bd = (rb // N_C) == (cb // N_C)
    eye = rb == cb
    zeros_l = jnp.zeros((L, LANE), BF16)

    pairs = range(N_PAIR)
    chains = [(p, hh) for p in pairs for hh in range(2)]
    sls = [slice(p * LANE, (p + 1) * LANE) for p in pairs]
    vms = [vm[:, sl].astype(BF16) for sl in sls]
    a_cat = []
    for p in pairs:
        alm_p, rhm_p = al_m[:, sls[p]], rho_m[:, sls[p]]
        lhs4 = jnp.concatenate([jnp.where(lo1, alm_p, 0.0), jnp.where(lo1, 0.0, alm_p),
                                jnp.where(lo1, rhm_p, 0.0), jnp.where(lo1, 0.0, rhm_p)], axis=0)
        rhs2 = jnp.concatenate([be_m[:, sls[p]], ka_m[:, sls[p]]], axis=0)
        a_cat.append(jnp.where(tri4, _mm_nt(lhs4, rhs2), 0.0).astype(BF16))
    zv = [jnp.concatenate([zeros_l, vms[p]], axis=0) for p in pairs]
    akv = [_mm(jnp.where(lo1, 0.0, a_cat[p][hh * L:(hh + 1) * L]), zv[p]) for p, hh in chains]
    xs, apow = [], []
    for p in pairs:
        al_p = al[:, sls[p]]
        akv_p = jnp.where(lo1, akv[2 * p], akv[2 * p + 1])
        xs.append(jnp.where(lo1, al_p, pltpu.roll(akv_p, N_C, 1)))
        xs.append(jnp.where(lo1, pltpu.roll(al_p, N_C, 1), akv_p))
        for hh in range(2):
            apow.append(jnp.where(lo1, a_cat[p][hh * L:(hh + 1) * L], 0.0).astype(BF16))
    for step in range(6):
        if step < 5:
            res = [_mm(a[:, :N_C], jnp.concatenate([x.astype(BF16), a], axis=1)) for a, x in zip(apow, xs)]
            apow = [r[:, LANE:].astype(BF16) for r in res]
        else:
            res = [_mm(a[:, :N_C], x) for a, x in zip(apow, xs)]
        xs = [x + r[:, :LANE] for x, r in zip(xs, res)]
    w_pq = []
    for p in pairs:
        x0, x1 = xs[2 * p], xs[2 * p + 1]
        p_pair = jnp.where(lo1, x0, pltpu.roll(x1, N_C, 1)).astype(BF16)
        q_pair = jnp.where(lo1, pltpu.roll(x0, N_C, 1), x1).astype(BF16)
        w_pq.append(jnp.concatenate([jnp.concatenate([p_pair, q_pair], axis=1),
                                     jnp.concatenate([zeros_l, vms[p]], axis=1)], axis=0))
    rbs = [_mm(a_cat[p][(2 + hh) * L:(3 + hh) * L], w_pq[p]) for p, hh in chains]
    gs = [_mm_tn(jnp.concatenate([b_hat[:, sls[p]], k_hat[:, sls[p]]], axis=0), w_pq[p]) for p in pairs]
    for p in pairs:
        sl = sls[p]
        r0, r1 = rbs[2 * p], rbs[2 * p + 1]
        rh_ref[:, sl] = (rho[:, sl] + jnp.where(lo1, r0[:, :LANE], r1[:, :LANE])).astype(rh_ref.dtype)
        yl_ref[:, sl] = jnp.where(lo1, r0[:, LANE:], r1[:, LANE:])
        d_p = jnp.broadcast_to(d_row[:, sl], (LANE, LANE))
        mt_ref[0, p] = (jnp.where(eye, d_p, 0.0) + jnp.where(bd, gs[p][:, :LANE], 0.0)).astype(mt_ref.dtype)
        nt_ref[0, p] = jnp.where(bd, gs[p][:, LANE:], 0.0)


def _rwkv_chunks(P, mu3, muwa, w0, wup, a0, aup, k_k, k_a, r_k):
    S = P.shape[0]
    nc = S // CHUNK
    nb = D_BR // LANE
    prev = lambda c: jnp.maximum(c * (CHUNK // SUB) - 1, 0)
    row = lambda n: pl.BlockSpec((1, n), lambda c: (0, 0))
    big = pl.BlockSpec((CHUNK, D_BR), lambda c: (c, 0))
    mat = pl.BlockSpec((1, N_PAIR, LANE, LANE), lambda c: (c, 0, 0, 0))
    return pl.pallas_call(
        _rwkv_chunk_kernel,
        grid=(nc,),
        in_specs=[pl.BlockSpec((CHUNK, D_BR), lambda c: (c, PC_CR // nb)),
                  pl.BlockSpec((CHUNK, D_BR), lambda c: (c, PC_CK // nb)),
                  pl.BlockSpec((CHUNK, D_BR), lambda c: (c, PC_CV // nb)),
                  pl.BlockSpec((CHUNK, LANE), lambda c: (c, PC_WA)),
                  pl.BlockSpec((SUB, D_BR), lambda c: (prev(c), PC_CR // nb)),
                  pl.BlockSpec((SUB, D_BR), lambda c: (prev(c), PC_CK // nb)),
                  pl.BlockSpec((SUB, D_BR), lambda c: (prev(c), PC_CV // nb)),
                  pl.BlockSpec((SUB, LANE), lambda c: (prev(c), PC_WA)),
                  pl.BlockSpec((3, D_BR), lambda c: (0, 0)),
                  row(LANE), row(D_BR),
                  pl.BlockSpec((LANE, D_BR), lambda c: (0, 0)),
                  row(D_BR),
                  pl.BlockSpec((LANE, D_BR), lambda c: (0, 0)),
                  row(D_BR), row(D_BR), row(D_BR)],
        out_specs=[mat, mat, big, big, big],
        out_shape=[jax.ShapeDtypeStruct((nc, N_PAIR, LANE, LANE), BF16),
                   jax.ShapeDtypeStruct((nc, N_PAIR, LANE, LANE), F32),
                   jax.ShapeDtypeStruct((S, D_BR), BF16),
                   jax.ShapeDtypeStruct((S, D_BR), F32),
                   jax.ShapeDtypeStruct((S, D_BR), F32)],
        compiler_params=_cparams(("parallel",)),
        name="rwkv_chunks",
    )(P, P, P, P, P, P, P, P, mu3, muwa, w0, wup, a0, aup, k_k, k_a, r_k)


def _rwkv_seq_kernel(mt_ref, nt_ref, rh_ref, yl_ref, bonus_ref, gc_ref,
                     lnw_ref, lnb_ref, o_ref, z_ref, y_ref, *, NCH):
    i = pl.program_id(0)

    @pl.when(i == 0)
    def _():
        z_ref[...] = jnp.zeros_like(z_ref)

    zs = [z_ref[p] for p in range(N_PAIR)]
    for c in range(NCH):
        rows = slice(c * CHUNK, (c + 1) * CHUNK)
        zb = [z.astype(BF16) for z in zs]
        zs = [jnp.dot(mt_ref[c, p], zb[p], preferred_element_type=F32) + nt_ref[c, p]
              for p in range(N_PAIR)]
        for p in range(N_PAIR):
            sl = slice(p * LANE, (p + 1) * LANE)
            y_ref[rows, sl] = jnp.dot(rh_ref[rows, sl], zb[p], preferred_element_type=F32) + yl_ref[rows, sl]
    for p in range(N_PAIR):
        z_ref[p] = zs[p]

    y = y_ref[...]
    inv_n = 1.0 / N_C
    mu = _head_sum(y) * inv_n
    yc = y - mu
    var = _head_sum(yc * yc) * inv_n
    yn = yc * lax.rsqrt(var + GN_EPS) * lnw_ref[...] + lnb_ref[...]
    o_ref[...] = ((yn + bonus_ref[...]) * _silu(gc_ref[...])).astype(o_ref.dtype)


def _rwkv_seq(mt, nt, rh, yl, bonus, P, ln_w, ln_b):
    S = rh.shape[0]
    T = min(256, S)
    nch = T // CHUNK
    nb = D_BR // LANE
    big = pl.BlockSpec((T, D_BR), lambda i: (i, 0))
    mat = pl.BlockSpec((nch, N_PAIR, LANE, LANE), lambda i: (i, 0, 0, 0))
    row = pl.BlockSpec((1, D_BR), lambda i: (0, 0))
    return pl.pallas_call(
        functools.partial(_rwkv_seq_kernel, NCH=nch),
        grid=(S // T,),
        in_specs=[mat, mat, big, big, big,
                  pl.BlockSpec((T, D_BR), lambda i: (i, PC_GC // nb)),
                  row, row],
        out_specs=big,
        out_shape=jax.ShapeDtypeStruct((S, D_BR), BF16),
        scratch_shapes=[pltpu.VMEM((N_PAIR, LANE, LANE), F32),
                        pltpu.VMEM((T, D_BR), F32)],
        compiler_params=_cparams(("arbitrary",)),
        name="rwkv_seq",
    )(mt, nt, rh, yl, bonus, P, ln_w, ln_b)


def _merge_kernel(h_ref, wg0_ref, wg1_ref, wg2_ref, bm_ref, ya_ref, yb_ref, yc_ref,
                  wb0_ref, wb1_ref, wb2_ref, o_ref):
    h = h_ref[...]
    bm = bm_ref[...]
    acc = None
    for n, (wg_ref, y_ref, wb_ref) in enumerate(((wg0_ref, ya_ref, wb0_ref),
                                                  (wg1_ref, yb_ref, wb1_ref),
                                                  (wg2_ref, yc_ref, wb2_ref))):
        gate = _sigmoid(jnp.dot(h, wg_ref[...], preferred_element_type=F32) + bm[n:n + 1, :])
        term = gate * jnp.dot(y_ref[...], wb_ref[...], preferred_element_type=F32)
        acc = term if acc is None else acc + term
    o_ref[...] = acc.astype(o_ref.dtype)


def _merge(h, w_mg, b_merge, ya, yb, yc, w_br):
    S = h.shape[0]
    tm = min(512, S)
    tn = 512
    nj = D_MODEL // tn
    ysp = pl.BlockSpec((tm, D_BR), lambda i, j: (i, 0))
    return pl.pallas_call(
        _merge_kernel,
        grid=(S // tm, nj),
        in_specs=[pl.BlockSpec((tm, D_MODEL), lambda i, j: (i, 0)),
                  pl.BlockSpec((D_MODEL, tn), lambda i, j: (0, j)),
                  pl.BlockSpec((D_MODEL, tn), lambda i, j: (0, nj + j)),
                  pl.BlockSpec((D_MODEL, tn), lambda i, j: (0, 2 * nj + j)),
                  pl.BlockSpec((3, tn), lambda i, j: (0, j)),
                  ysp, ysp, ysp,
                  pl.BlockSpec((None, D_BR, tn), lambda i, j: (0, 0, j)),
                  pl.BlockSpec((None, D_BR, tn), lambda i, j: (1, 0, j)),
                  pl.BlockSpec((None, D_BR, tn), lambda i, j: (2, 0, j))],
        out_specs=pl.BlockSpec((tm, tn), lambda i, j: (i, j)),
        out_shape=jax.ShapeDtypeStruct((S, D_MODEL), BF16),
        compiler_params=_cparams(("parallel", "parallel")),
        name="merge",
    )(h, w_mg, w_mg, w_mg, b_merge, ya, yb, yc, w_br, w_br, w_br)


def _out_kernel(m_ref, w_ref, x_ref, g_ref, o_ref):
    y = jnp.dot(m_ref[...], w_ref[...], preferred_element_type=F32)
    yn = y * lax.rsqrt(jnp.mean(y * y, axis=-1, keepdims=True) + NORM_EPS)
    o_ref[...] = x_ref[...] + yn * g_ref[...]


def _out_proj(m, w_out, x, g):
    S = m.shape[0]
    tm = min(512, S)
    blk = pl.BlockSpec((tm, D_MODEL), lambda i: (i, 0))
    return pl.pallas_call(
        _out_kernel,
        grid=(S // tm,),
        in_specs=[blk,
                  pl.BlockSpec((D_MODEL, D_MODEL), lambda i: (0, 0)),
                  blk,
                  pl.BlockSpec((1, D_MODEL), lambda i: (0, 0))],
        out_specs=blk,
        out_shape=jax.ShapeDtypeStruct((S, D_MODEL), F32),
        compiler_params=_cparams(("parallel",)),
        name="out_proj",
    )(m, w_out, x, g.reshape(1, D_MODEL))


def _layer_weights(w_in, lru_gate_w, rwkv_w_up, rwkv_a_up):
    o = 0
    xa_w, ga_w = w_in[:, o:o + D_BR], w_in[:, o + D_BR:o + 2 * D_BR]
    o += 2 * D_BR
    qkv_w = w_in[:, o:o + 3 * D_BR]
    fl_w = w_in[:, o + 3 * D_BR:o + 3 * D_BR + H_B]
    gb_w = w_in[:, o + 3 * D_BR + H_B:o + 4 * D_BR + H_B]
    o += 4 * D_BR + H_B
    pc_w = w_in[:, o:o + 3 * D_BR + 2 * LORA]
    o += 3 * D_BR + 2 * LORA
    gc_w = w_in[:, o:o + D_BR]
    o += D_BR
    mg_w = w_in[:, o:]
    pad = jnp.zeros((D_MODEL, P_COLS - (PC_FL * LANE + H_B)), F32)
    w_p = jnp.concatenate([xa_w, ga_w, gb_w, gc_w, pc_w, fl_w, pad], axis=1).astype(BF16)
    eye = jnp.eye(H_A, dtype=F32)
    wg = jnp.einsum('ghij,hk->ghikj', lru_gate_w, eye).reshape(2, D_BR, D_BR)
    wg = jnp.concatenate([wg[0], wg[1]], axis=1).astype(BF16)
    z = jnp.zeros((LORA, D_BR), F32)
    wup = jnp.concatenate([rwkv_w_up, z], axis=0).astype(BF16)
    aup = jnp.concatenate([z, rwkv_a_up], axis=0).astype(BF16)
    return w_p, qkv_w.astype(BF16), mg_w.astype(BF16), wg, wup, aup


def kernel(x, pre_norm_w, post_norm_w, w_in, b_merge, conv_w, conv_b, lru_gate_w, lru_gate_b, lru_lambda,
           fox_b_f, rwkv_mu, rwkv_w0, rwkv_w_up, rwkv_a0, rwkv_a_up, rwkv_k_k, rwkv_k_a, rwkv_r_k,
           rwkv_ln_w, rwkv_ln_b, w_branch, w_out):
    B, S, _ = x.shape
    assert B == 1 and S % CHUNK == 0
    depth = w_in.shape[0]
    xs = x.reshape(S, D_MODEL)
    ones_p = jnp.ones((1, P_COLS), F32)
    qscale = jnp.concatenate([jnp.full((1, D_BR), DH_B ** -0.5 * LOG2E, F32), jnp.ones((1, 2 * D_BR), F32)], axis=1)
    row = lambda a: a.reshape(1, -1)
    for l in range(depth):
        w_p, w_qkv, w_mg, wg, wup, aup = _layer_weights(w_in[l], lru_gate_w[l], rwkv_w_up[l], rwkv_a_up[l])
        h = _rmsnorm_bf16(xs, pre_norm_w[l])
        P = _proj(h, w_p, ones_p, F32, 512)
        qkv = _proj(h, w_qkv, qscale, BF16, 512)
        ya = _rglru(P, conv_w[l], row(conv_b[l]), wg, row(lru_gate_b[l]), row(lru_lambda[l]))
        bf_pad = jnp.concatenate([fox_b_f[l], jnp.zeros((LANE - H_B,), F32)]).reshape(1, LANE)
        kx, cs = _fox_c(P, bf_pad)
        yb = _fox(qkv, kx, cs, P)
        mu = rwkv_mu[l]
        mu3 = mu[:3 * D_BR].reshape(3, D_BR)
        muwa = mu[3 * D_BR:].reshape(1, LANE)
        mt, nt, rh, yl, bonus = _rwkv_chunks(P, mu3, muwa, row(rwkv_w0[l]), wup, row(rwkv_a0[l]), aup,
                                             row(rwkv_k_k[l]), row(rwkv_k_a[l]), row(rwkv_r_k[l]))
        yc = _rwkv_seq(mt, nt, rh, yl, bonus, P, row(rwkv_ln_w[l]), row(rwkv_ln_b[l]))
        m = _merge(h, w_mg, b_merge[l], ya, yb, yc, w_branch[l].astype(BF16))
        xs = _out_proj(m, w_out[l].astype(BF16), xs, post_norm_w[l])
    return xs.reshape(B, S, D_MODEL)
```

```python
import functools

import jax
import jax.numpy as jnp
from jax import lax
from jax.experimental import pallas as pl
from jax.experimental.pallas import tpu as pltpu

F32 = jnp.float32
BF16 = jnp.bfloat16

D_MODEL = 2048
D_BR = 1024
H_A, BW_A = 16, 64
CONV_W = 4
LRU_C = 8.0
H_B, DH_B = 8, 128
H_C, N_C = 16, 64
LORA = 64
NORM_EPS = 1e-6
GN_EPS = 64e-5
LANE = 128
SUB = 8
CHUNK = 64
N_PAIR = H_C // 2
NEG = -1e30
LOG2E = 1.4426950408889634
FOX_Q = 1024
FOX_T = 1024
FOX_NB = 2
FOX_VMEM = 56 * 1024 * 1024

PC_XA, PC_GA, PC_GB, PC_GC, PC_CR, PC_CK, PC_CV = 0, 8, 16, 24, 32, 40, 48
PC_WA, PC_FL = 56, 57
P_COLS = 60 * LANE

VMEM_LIMIT = 48 * 1024 * 1024


def _cparams(sem):
    return pltpu.CompilerParams(dimension_semantics=sem, vmem_limit_bytes=VMEM_LIMIT)


def _sigmoid(x):
    return 0.5 * jnp.tanh(0.5 * x) + 0.5


def _silu(x):
    return x * _sigmoid(x)


def _softplus(z):
    return jnp.maximum(z, 0.0) + jnp.log1p(jnp.exp(-jnp.abs(z)))


def _split3(x):
    hi = x.astype(BF16)
    r1 = x - hi.astype(F32)
    mid = r1.astype(BF16)
    return hi, mid, (r1 - mid.astype(F32)).astype(BF16)


def _tri_cumsum(lower_mask, x):
    tri = lower_mask.astype(BF16)
    hi, mid, lo = _split3(x)
    return (jnp.dot(tri, hi, preferred_element_type=F32) + jnp.dot(tri, mid, preferred_element_type=F32)
            + jnp.dot(tri, lo, preferred_element_type=F32))


def _norm_kernel(x_ref, g_ref, o_ref):
    x = x_ref[...]
    y = x * lax.rsqrt(jnp.mean(x * x, axis=-1, keepdims=True) + NORM_EPS)
    o_ref[...] = (y * g_ref[...]).astype(o_ref.dtype)


def _rmsnorm_bf16(x, g):
    S = x.shape[0]
    tm = min(512, S)
    return pl.pallas_call(
        _norm_kernel,
        grid=(S // tm,),
        in_specs=[pl.BlockSpec((tm, D_MODEL), lambda i: (i, 0)),
                  pl.BlockSpec((1, D_MODEL), lambda i: (0, 0))],
        out_specs=pl.BlockSpec((tm, D_MODEL), lambda i: (i, 0)),
        out_shape=jax.ShapeDtypeStruct((S, D_MODEL), BF16),
        compiler_params=_cparams(("parallel",)),
        name="rmsnorm",
    )(x, g.reshape(1, D_MODEL))


def _proj_kernel(h_ref, w_ref, s_ref, o_ref):
    acc = jnp.dot(h_ref[...], w_ref[...], preferred_element_type=F32)
    o_ref[...] = (acc * s_ref[...]).astype(o_ref.dtype)


def _proj(h, w, colscale, out_dtype, tn):
    S, K = h.shape
    N = w.shape[1]
    tm = min(1024, S)
    return pl.pallas_call(
        _proj_kernel,
        grid=(S // tm, N // tn),
        in_specs=[pl.BlockSpec((tm, K), lambda i, j: (i, 0)),
                  pl.BlockSpec((K, tn), lambda i, j: (0, j)),
                  pl.BlockSpec((1, tn), lambda i, j: (0, j))],
        out_specs=pl.BlockSpec((tm, tn), lambda i, j: (i, j)),
        out_shape=jax.ShapeDtypeStruct((S, N), out_dtype),
        compiler_params=_cparams(("parallel", "parallel")),
        name="in_proj",
    )(h, w, colscale)


def _rglru_kernel(xa_ref, ga_ref, cw_ref, cb_ref, wg_ref, gbias_ref, lam_ref, o_ref,
                  xe_ref, hc_ref, a_ref, b_ref, *, T):
    i = pl.program_id(0)

    @pl.when(i == 0)
    def _():
        xe_ref[...] = jnp.zeros_like(xe_ref)
        hc_ref[...] = jnp.zeros_like(hc_ref)

    G = T // SUB
    sub = lax.broadcasted_iota(jnp.int32, (G, SUB, D_BR), 1)

    def rot(v, d):
        return pltpu.roll(v, d, 1)

    x = xa_ref[...].reshape(G, SUB, D_BR)
    x_prev = jnp.concatenate([xe_ref[...].reshape(1, SUB, D_BR), x[:G - 1]], axis=0)
    xe_ref[...] = xa_ref[T - SUB:T, :]
    cw = cw_ref[...]
    xc = cb_ref[...] + x * cw[3:4]
    for d in range(1, CONV_W):
        xc = xc + jnp.where(sub < d, rot(x_prev, d), rot(x, d)) * cw[3 - d:4 - d]
    xc = xc.reshape(T, D_BR)

    gates = jnp.dot(xc.astype(BF16), wg_ref[...], preferred_element_type=F32) + gbias_ref[...]
    r = _sigmoid(gates[:, :D_BR])
    ig = _sigmoid(gates[:, D_BR:])
    log_a = (-LRU_C) * r * _softplus(-lam_ref[...])
    a = jnp.exp(log_a)
    y = -jnp.tanh(log_a) * (1.0 + a * a)
    mult = jnp.where(y > 0.0, y * lax.rsqrt(y), 0.0)
    rows = lax.broadcasted_iota(jnp.int32, (T, D_BR), 0)
    mult = jnp.where((rows + i * T) == 0, 1.0, mult)
    b = (mult * ig * xc).reshape(G, SUB, D_BR)
    a = a.reshape(G, SUB, D_BR)

    for d in (1, 2, 4):
        keep = sub >= d
        b = jnp.where(keep, a * rot(b, d) + b, b)
        a = jnp.where(keep, a * rot(a, d), a)
    a_ref[...] = a.reshape(T, D_BR)
    b_ref[...] = b.reshape(T, D_BR)
    carry = hc_ref[...]
    for g in range(T // SUB):
        sl = slice(g * SUB, (g + 1) * SUB)
        hg = b_ref[sl, :] + a_ref[sl, :] * carry
        b_ref[sl, :] = hg
        carry = jnp.broadcast_to(hg[SUB - 1:SUB, :], (SUB, D_BR))
    hc_ref[...] = carry
    o_ref[...] = (b_ref[...] * _silu(ga_ref[...])).astype(o_ref.dtype)


def _rglru(P, conv_w, conv_b, wg, gbias, lam):
    S = P.shape[0]
    T = min(256, S)
    nb = D_BR // LANE
    return pl.pallas_call(
        functools.partial(_rglru_kernel, T=T),
        grid=(S // T,),
        in_specs=[pl.BlockSpec((T, D_BR), lambda i: (i, PC_XA // nb)),
                  pl.BlockSpec((T, D_BR), lambda i: (i, PC_GA // nb)),
                  pl.BlockSpec((CONV_W, D_BR), lambda i: (0, 0)),
                  pl.BlockSpec((1, D_BR), lambda i: (0, 0)),
                  pl.BlockSpec((D_BR, 2 * D_BR), lambda i: (0, 0)),
                  pl.BlockSpec((1, 2 * D_BR), lambda i: (0, 0)),
                  pl.BlockSpec((1, D_BR), lambda i: (0, 0))],
        out_specs=pl.BlockSpec((T, D_BR), lambda i: (i, 0)),
        out_shape=jax.ShapeDtypeStruct((S, D_BR), BF16),
        scratch_shapes=[pltpu.VMEM((SUB, D_BR), F32),
                        pltpu.VMEM((SUB, D_BR), F32),
                        pltpu.VMEM((T, D_BR), F32),
                        pltpu.VMEM((T, D_BR), F32)],
        compiler_params=_cparams(("arbitrary",)),
        name="rglru",
    )(P, P, conv_w, conv_b, wg, gbias, lam)


def _fox_c_kernel(fl_ref, bf_ref, kx_ref, cs_ref, carry_ref, *, T):
    i = pl.program_id(0)

    @pl.when(i == 0)
    def _():
        carry_ref[...] = jnp.zeros_like(carry_ref)

    z = fl_ref[...] + bf_ref[...]
    lf = (jnp.minimum(z, 0.0) - jnp.log1p(jnp.exp(-jnp.abs(z)))) * LOG2E
    rr = lax.broadcasted_iota(jnp.int32, (T, T), 0)
    cc = lax.broadcasted_iota(jnp.int32, (T, T), 1)
    lc = _tri_cumsum(cc <= rr, lf)
    carry = carry_ref[...]
    cs_ref[0] = carry
    carry_ref[...] = carry + jnp.broadcast_to(lc[T - 1:T, :], (SUB, LANE))
    pieces = jnp.concatenate(_split3(-lc), axis=1)
    rs = lax.broadcasted_iota(jnp.int32, (3 * LANE, D_BR), 0)
    cs = lax.broadcasted_iota(jnp.int32, (3 * LANE, D_BR), 1)
    sel = (cs == (rs % LANE) * LANE + rs // LANE).astype(BF16)
    kx_ref[...] = jnp.dot(pieces, sel, preferred_element_type=F32).astype(BF16)


def _fox_c(P, bf_pad):
    S = P.shape[0]
    T = min(FOX_T, FOX_Q, S // 2)
    return pl.pallas_call(
        functools.partial(_fox_c_kernel, T=T),
        grid=(S // T,),
        in_specs=[pl.BlockSpec((T, LANE), lambda i: (i, PC_FL)),
                  pl.BlockSpec((1, LANE), lambda i: (0, 0))],
        out_specs=[pl.BlockSpec((T, D_BR), lambda i: (i, 0)),
                   pl.BlockSpec((1, SUB, LANE), lambda i: (i, 0, 0))],
        out_shape=[jax.ShapeDtypeStruct((S, D_BR), BF16),
                   jax.ShapeDtypeStruct((S // T, SUB, LANE), F32)],
        scratch_shapes=[pltpu.VMEM((SUB, LANE), F32)],
        compiler_params=_cparams(("arbitrary",)),
        name="fox_cumgate",
    )(P, bf_pad)


def _fox_kernel(cs_ref, q_ref, k_ref, kx_ref, v_ref, gb_ref, o_ref, sh_ref, dp_ref, acc_ref, *, Q, T, NB):
    h = pl.program_id(0)
    i = pl.program_id(1)
    assert Q == T and NB == 2
    lane = lax.broadcasted_iota(jnp.int32, (Q, DH_B), 1)
    qa = jnp.concatenate([q_ref[...], jnp.where(lane < 3, 1.0, 0.0).astype(BF16)], axis=1)
    ones_v = jnp.ones((T, DH_B), BF16)
    c_i = cs_ref[h, i]

    def chunk(j):
        ks = pl.multiple_of(j * T, T)
        kc = jnp.concatenate([k_ref[pl.ds(ks, T), :], kx_ref[pl.ds(ks, T), :]], axis=1)
        vc = jnp.concatenate([v_ref[pl.ds(ks, T), :], ones_v], axis=1)
        return kc, vc, c_i - cs_ref[h, j]

    def qk(kc):
        return lax.dot_general(qa, kc, (((1,), (1,)), ((), ())), preferred_element_type=F32)

    def sm_pv(carry, s, vc, d):
        sh, d_prev, acc = carry
        a = sh + (d_prev - d)
        sh_new = jnp.maximum(a, jnp.max(s, axis=-1, keepdims=True))
        p = jnp.exp2(s - sh_new)
        acc = jnp.exp2(a - sh_new) * acc + jnp.dot(p.astype(BF16), vc, preferred_element_type=F32)
        return sh_new, d, acc

    def chunks(j0, n, carry, diag=False):
        cks = [chunk(j0 + c) for c in range(n)]
        s_next = qk(cks[0][0])
        for c in range(n):
            s = s_next
            if c + 1 < n:
                s_next = qk(cks[c + 1][0])
            elif diag:
                rows = lax.broadcasted_iota(jnp.int32, (Q, T), 0)
                cols = lax.broadcasted_iota(jnp.int32, (Q, T), 1)
                s = jnp.where(cols <= rows, s, NEG)
            carry = sm_pv(carry, s, cks[c][1], cks[c][2])
        return carry

    carry = (jnp.full((Q, 1), NEG, F32), jnp.float32(0.0), jnp.zeros((Q, 2 * DH_B), F32))
    n = i + 1
    tail = jnp.where(n % 2 == 0, 2, jnp.where(n == 1, 1, 3))
    sh, d_prev, acc = lax.fori_loop(0, (n - tail) // 2, lambda jj, c: chunks(2 * jj, 2, c), carry)
    sh_ref[...] = sh
    dp_ref[0] = d_prev
    acc_ref[...] = acc
    for t in (1, 2, 3):
        @pl.when(tail == t)
        def _():
            acc_ref[...] = chunks(n - t, t, (sh_ref[...], dp_ref[0], acc_ref[...]), diag=True)[2]
    acc = acc_ref[...]
    o_ref[...] = (acc[:, :DH_B] / acc[:, DH_B:] * _silu(gb_ref[...])).astype(o_ref.dtype)


def _fox(qkv, kx, cs, P):
    S = qkv.shape[0]
    Q = min(FOX_Q, S // 2)
    T = min(FOX_T, Q)
    cst = cs[:, 0, :H_B].T
    kv = lambda col0: pl.BlockSpec((S, DH_B), lambda h, i: (0, col0 + h))
    return pl.pallas_call(
        functools.partial(_fox_kernel, Q=Q, T=T, NB=FOX_NB),
        grid=(H_B, S // Q),
        in_specs=[pl.BlockSpec(memory_space=pltpu.SMEM),
                  pl.BlockSpec((Q, DH_B), lambda h, i: (i, h)),
                  kv(H_B), kv(0), kv(2 * H_B),
                  pl.BlockSpec((Q, DH_B), lambda h, i: (i, PC_GB + h))],
        out_specs=pl.BlockSpec((Q, DH_B), lambda h, i: (i, h)),
        out_shape=jax.ShapeDtypeStruct((S, D_BR), BF16),
        scratch_shapes=[pltpu.VMEM((Q, 1), F32), pltpu.SMEM((1,), F32),
                        pltpu.VMEM((Q, 2 * DH_B), F32)],
        compiler_params=pltpu.CompilerParams(dimension_semantics=("parallel", "parallel"),
                                             vmem_limit_bytes=FOX_VMEM),
        name="fox_attn",
    )(cst, qkv, qkv, kx, qkv, P)


def _head_sum(x):
    rr = lax.broadcasted_iota(jnp.int32, (LANE, LANE), 0) // N_C
    cc = lax.broadcasted_iota(jnp.int32, (LANE, LANE), 1) // N_C
    ones_bd = (rr == cc).astype(BF16)
    hi = x.astype(BF16)
    lo = (x - hi.astype(F32)).astype(BF16)
    outs = []
    for p in range(N_PAIR):
        sl = slice(p * LANE, (p + 1) * LANE)
        outs.append(jnp.dot(hi[:, sl], ones_bd, preferred_element_type=F32)
                    + jnp.dot(lo[:, sl], ones_bd, preferred_element_type=F32))
    return jnp.concatenate(outs, axis=1)


def _rwkv_tokens(cr_ref, ck_ref, cv_ref, wa_ref, pr_ref, pk_ref, pv_ref, pwa_ref,
                 mu_ref, muwa_ref, w0_ref, wup_ref, a0_ref, aup_ref, kk_ref, ka_ref):
    live = (pl.program_id(0) > 0).astype(F32)

    def shift_mix(cur_ref, prev_ref, mu):
        cur = cur_ref[...]
        rows = lax.broadcasted_iota(jnp.int32, cur.shape, 0)
        first = jnp.broadcast_to(prev_ref[SUB - 1:SUB, :] * live, cur.shape)
        prev = jnp.where(rows == 0, first, pltpu.roll(cur, 1, 0))
        return cur + (prev - cur) * mu

    r = shift_mix(cr_ref, pr_ref, mu_ref[0:1, :])
    kraw = shift_mix(ck_ref, pk_ref, mu_ref[1:2, :])
    v = shift_mix(cv_ref, pv_ref, mu_ref[2:3, :])
    wa = shift_mix(wa_ref, pwa_ref, muwa_ref[...])
    zw = w0_ref[...] + jnp.dot(jnp.tanh(wa).astype(BF16), wup_ref[...], preferred_element_type=F32)
    za = a0_ref[...] + jnp.dot(wa.astype(BF16), aup_ref[...], preferred_element_type=F32)
    g = (-0.6065306597126334) * _sigmoid(zw)
    a = _sigmoid(za)
    kk = kraw * kk_ref[...]
    ss = _head_sum(kk * kk)
    kkn = kk * lax.rsqrt(jnp.maximum(ss, 1e-24))
    return r, kraw * (1.0 + (a - 1.0) * ka_ref[...]), v, kkn, kkn * a, g


def _mm(a, b):
    return jnp.dot(a.astype(BF16), b.astype(BF16), preferred_element_type=F32)


def _mm_nt(a, b):
    return lax.dot_general(a.astype(BF16), b.astype(BF16), (((1,), (1,)), ((), ())),
                           preferred_element_type=F32)


def _mm_tn(a, b):
    return lax.dot_general(a.astype(BF16), b.astype(BF16), (((0,), (0,)), ((), ())),
                           preferred_element_type=F32)


def _rwkv_chunk_kernel(*refs):
    rk_ref, mt_ref, nt_ref, rh_ref, yl_ref, bonus_ref = refs[16:]
    L = CHUNK
    rv, kv, vm, kkn, bv, g = _rwkv_tokens(*refs[:16])
    bonus_ref[...] = _head_sum(rv * kv * rk_ref[...]) * vm
    rr = lax.broadcasted_iota(jnp.int32, (L, L), 0)
    cc = lax.broadcasted_iota(jnp.int32, (L, L), 1)
    G = _tri_cumsum(cc <= rr, g)
    Ge = G - g
    Gm = G[L // 2 - 1:L // 2, :]
    GL = G[L - 1:L, :]
    na = -kkn
    al = na * jnp.exp(Ge)
    al_m = na * jnp.exp(Ge - Gm)
    e_b = jnp.exp(Gm - G)
    be_m = bv * e_b
    ka_m = kv * e_b
    rho = rv * jnp.exp(G)
    rho_m = rv * jnp.exp(G - Gm)
    e_l = jnp.exp(GL - G)
    b_hat = bv * e_l
    k_hat = kv * e_l
    d_row = jnp.exp(GL)

    lo1 = lax.broadcasted_iota(jnp.int32, (L, LANE), 1) < N_C
    r4 = lax.broadcasted_iota(jnp.int32, (4 * L, LANE), 0)
    c4 = lax.broadcasted_iota(jnp.int32, (4 * L, LANE), 1) & (N_C - 1)
    tri4 = c4 < (r4 & (L - 1)) + jnp.where(r4 < 2 * L, 0, 1)
    rb = lax.broadcasted_iota(jnp.int32, (LANE, LANE), 0)
    cb = lax.broadcasted_iota(jnp.int32, (LANE, LANE), 1)
    bd = (rb // N_C) == (cb // N_C)
    eye = rb == cb
    zeros_l = jnp.zeros((L, LANE), BF16)

    pairs = range(N_PAIR)
    chains = [(p, hh) for p in pairs for hh in range(2)]
    sls = [slice(p * LANE, (p + 1) * LANE) for p in pairs]
    vms = [vm[:, sl].astype(BF16) for sl in sls]
    a_cat = []
    for p in pairs:
        alm_p, rhm_p = al_m[:, sls[p]], rho_m[:, sls[p]]
        lhs4 = jnp.concatenate([jnp.where(lo1, alm_p, 0.0), jnp.where(lo1, 0.0, alm_p),
                                jnp.where(lo1, rhm_p, 0.0), jnp.where(lo1, 0.0, rhm_p)], axis=0)
        rhs2 = jnp.concatenate([be_m[:, sls[p]], ka_m[:, sls[p]]], axis=0)
        a_cat.append(jnp.where(tri4, _mm_nt(lhs4, rhs2), 0.0).astype(BF16))
    zv = [jnp.concatenate([zeros_l, vms[p]], axis=0) for p in pairs]
    akv = [_mm(jnp.where(lo1, 0.0, a_cat[p][hh * L:(hh + 1) * L]), zv[p]) for p, hh in chains]
    xs, apow = [], []
    for p in pairs:
        al_p = al[:, sls[p]]
        akv_p = jnp.where(lo1, akv[2 * p], akv[2 * p + 1])
        xs.append(jnp.where(lo1, al_p, pltpu.roll(akv_p, N_C, 1)))
        xs.append(jnp.where(lo1, pltpu.roll(al_p, N_C, 1), akv_p))
        for hh in range(2):
            apow.append(jnp.where(lo1, a_cat[p][hh * L:(hh + 1) * L], 0.0).astype(BF16))
    for step in range(6):
        if step < 5:
            res = [_mm(a[:, :N_C], jnp.concatenate([x.astype(BF16), a], axis=1)) for a, x in zip(apow, xs)]
            apow = [r[:, LANE:].astype(BF16) for r in res]
        else:
            res = [_mm(a[:, :N_C], x) for a, x in zip(apow, xs)]
        xs = [x + r[:, :LANE] for x, r in zip(xs, res)]
    w_pq = []
    for p in pairs:
        x0, x1 = xs[2 * p], xs[2 * p + 1]
        p_pair = jnp.where(lo1, x0, pltpu.roll(x1, N_C, 1)).astype(BF16)
        q_pair = jnp.where(lo1, pltpu.roll(x0, N_C, 1), x1).astype(BF16)
        w_pq.append(jnp.concatenate([jnp.concatenate([p_pair, q_pair], axis=1),
                                     jnp.concatenate([zeros_l, vms[p]], axis=1)], axis=0))
    rbs = [_mm(a_cat[p][(2 + hh) * L:(3 + hh) * L], w_pq[p]) for p, hh in chains]
    gs = [_mm_tn(jnp.concatenate([b_hat[:, sls[p]], k_hat[:, sls[p]]], axis=0), w_pq[p]) for p in pairs]
    for p in pairs:
        sl = sls[p]
        r0, r1 = rbs[2 * p], rbs[2 * p + 1]
        rh_ref[:, sl] = (rho[:, sl] + jnp.where(lo1, r0[:, :LANE], r1[:, :LANE])).astype(rh_ref.dtype)
        yl_ref[:, sl] = jnp.where(lo1, r0[:, LANE:], r1[:, LANE:])
        d_p = jnp.broadcast_to(d_row[:, sl], (LANE, LANE))
        mt_ref[0, p] = (jnp.where(eye, d_p, 0.0) + jnp.where(bd, gs[p][:, :LANE], 0.0)).astype(mt_ref.dtype)
        nt_ref[0, p] = jnp.where(bd, gs[p][:, LANE:], 0.0)


def _rwkv_chunks(P, mu3, muwa, w0, wup, a0, aup, k_k, k_a, r_k):
    S = P.shape[0]
    nc = S // CHUNK
    nb = D_BR // LANE
    prev = lambda c: jnp.maximum(c * (CHUNK // SUB) - 1, 0)
    row = lambda n: pl.BlockSpec((1, n), lambda c: (0, 0))
    big = pl.BlockSpec((CHUNK, D_BR), lambda c: (c, 0))
    mat = pl.BlockSpec((1, N_PAIR, LANE, LANE), lambda c: (c, 0, 0, 0))
    return pl.pallas_call(
        _rwkv_chunk_kernel,
        grid=(nc,),
        in_specs=[pl.BlockSpec((CHUNK, D_BR), lambda c: (c, PC_CR // nb)),
                  pl.BlockSpec((CHUNK, D_BR), lambda c: (c, PC_CK // nb)),
                  pl.BlockSpec((CHUNK, D_BR), lambda c: (c, PC_CV // nb)),
                  pl.BlockSpec((CHUNK, LANE), lambda c: (c, PC_WA)),
                  pl.BlockSpec((SUB, D_BR), lambda c: (prev(c), PC_CR // nb)),
                  pl.BlockSpec((SUB, D_BR), lambda c: (prev(c), PC_CK // nb)),
                  pl.BlockSpec((SUB, D_BR), lambda c: (prev(c), PC_CV // nb)),
                  pl.BlockSpec((SUB, LANE), lambda c: (prev(c), PC_WA)),
                  pl.BlockSpec((3, D_BR), lambda c: (0, 0)),
                  row(LANE), row(D_BR),
                  pl.BlockSpec((LANE, D_BR), lambda c: (0, 0)),
                  row(D_BR),
                  pl.BlockSpec((LANE, D_BR), lambda c: (0, 0)),
                  row(D_BR), row(D_BR), row(D_BR)],
        out_specs=[mat, mat, big, big, big],
        out_shape=[jax.ShapeDtypeStruct((nc, N_PAIR, LANE, LANE), BF16),
                   jax.ShapeDtypeStruct((nc, N_PAIR, LANE, LANE), F32),
                   jax.ShapeDtypeStruct((S, D_BR), BF16),
                   jax.ShapeDtypeStruct((S, D_BR), F32),
                   jax.ShapeDtypeStruct((S, D_BR), F32)],
        compiler_params=_cparams(("parallel",)),
        name="rwkv_chunks",
    )(P, P, P, P, P, P, P, P, mu3, muwa, w0, wup, a0, aup, k_k, k_a, r_k)


def _rwkv_seq_kernel(mt_ref, nt_ref, rh_ref, yl_ref, bonus_ref, gc_ref,
                     lnw_ref, lnb_ref, o_ref, z_ref, y_ref, *, NCH):
    i = pl.program_id(0)

    @pl.when(i == 0)
    def _():
        z_ref[...] = jnp.zeros_like(z_ref)

    zs = [z_ref[p] for p in range(N_PAIR)]
    for c in range(NCH):
        rows = slice(c * CHUNK, (c + 1) * CHUNK)
        zb = [z.astype(BF16) for z in zs]
        zs = [jnp.dot(mt_ref[c, p], zb[p], preferred_element_type=F32) + nt_ref[c, p]
              for p in range(N_PAIR)]
        for p in range(N_PAIR):
            sl = slice(p * LANE, (p + 1) * LANE)
            y_ref[rows, sl] = jnp.dot(rh_ref[rows, sl], zb[p], preferred_element_type=F32) + yl_ref[rows, sl]
    for p in range(N_PAIR):
        z_ref[p] = zs[p]

    y = y_ref[...]
    inv_n = 1.0 / N_C
    mu = _head_sum(y) * inv_n
    yc = y - mu
    var = _head_sum(yc * yc) * inv_n
    yn = yc * lax.rsqrt(var + GN_EPS) * lnw_ref[...] + lnb_ref[...]
    o_ref[...] = ((yn + bonus_ref[...]) * _silu(gc_ref[...])).astype(o_ref.dtype)


def _rwkv_seq(mt, nt, rh, yl, bonus, P, ln_w, ln_b):
    S = rh.shape[0]
    T = min(256, S)
    nch = T // CHUNK
    nb = D_BR // LANE
    big = pl.BlockSpec((T, D_BR), lambda i: (i, 0))
    mat = pl.BlockSpec((nch, N_PAIR, LANE, LANE), lambda i: (i, 0, 0, 0))
    row = pl.BlockSpec((1, D_BR), lambda i: (0, 0))
    return pl.pallas_call(
        functools.partial(_rwkv_seq_kernel, NCH=nch),
        grid=(S // T,),
        in_specs=[mat, mat, big, big, big,
                  pl.BlockSpec((T, D_BR), lambda i: (i, PC_GC // nb)),
                  row, row],
        out_specs=big,
        out_shape=jax.ShapeDtypeStruct((S, D_BR), BF16),
        scratch_shapes=[pltpu.VMEM((N_PAIR, LANE, LANE), F32),
                        pltpu.VMEM((T, D_BR), F32)],
        compiler_params=_cparams(("arbitrary",)),
        name="rwkv_seq",
    )(mt, nt, rh, yl, bonus, P, ln_w, ln_b)


def _merge_kernel(h_ref, wg0_ref, wg1_ref, wg2_ref, bm_ref, ya_ref, yb_ref, yc_ref,
                  wb0_ref, wb1_ref, wb2_ref, o_ref):
    h = h_ref[...]
    bm = bm_ref[...]
    acc = None
    for n, (wg_ref, y_ref, wb_ref) in enumerate(((wg0_ref, ya_ref, wb0_ref),
                                                  (wg1_ref, yb_ref, wb1_ref),
                                                  (wg2_ref, yc_ref, wb2_ref))):
        gate = _sigmoid(jnp.dot(h, wg_ref[...], preferred_element_type=F32) + bm[n:n + 1, :])
        term = gate * jnp.dot(y_ref[...], wb_ref[...], preferred_element_type=F32)
        acc = term if acc is None else acc + term
    o_ref[...] = acc.astype(o_ref.dtype)


def _merge(h, w_mg, b_merge, ya, yb, yc, w_br):
    S = h.shape[0]
    tm = min(1024, S)
    tn = 512
    nj = D_MODEL // tn
    ysp = pl.BlockSpec((tm, D_BR), lambda i, j: (i, 0))
    return pl.pallas_call(
        _merge_kernel,
        grid=(S // tm, nj),
        in_specs=[pl.BlockSpec((tm, D_MODEL), lambda i, j: (i, 0)),
                  pl.BlockSpec((D_MODEL, tn), lambda i, j: (0, j)),
                  pl.BlockSpec((D_MODEL, tn), lambda i, j: (0, nj + j)),
                  pl.BlockSpec((D_MODEL, tn), lambda i, j: (0, 2 * nj + j)),
                  pl.BlockSpec((3, tn), lambda i, j: (0, j)),
                  ysp, ysp, ysp,
                  pl.BlockSpec((None, D_BR, tn), lambda i, j: (0, 0, j)),
                  pl.BlockSpec((None, D_BR, tn), lambda i, j: (1, 0, j)),
                  pl.BlockSpec((None, D_BR, tn), lambda i, j: (2, 0, j))],
        out_specs=pl.BlockSpec((tm, tn), lambda i, j: (i, j)),
        out_shape=jax.ShapeDtypeStruct((S, D_MODEL), BF16),
        compiler_params=pltpu.CompilerParams(dimension_semantics=("parallel", "parallel"),
                                             vmem_limit_bytes=FOX_VMEM),
        name="merge",
    )(h, w_mg, w_mg, w_mg, b_merge, ya, yb, yc, w_br, w_br, w_br)


def _out_kernel(m_ref, w_ref, x_ref, g_ref, o_ref):
    y = jnp.dot(m_ref[...], w_ref[...], preferred_element_type=F32)
    yn = y * lax.rsqrt(jnp.mean(y * y, axis=-1, keepdims=True) + NORM_EPS)
    o_ref[...] = x_ref[...] + yn * g_ref[...]


def _out_proj(m, w_out, x, g):
    S = m.shape[0]
    tm = min(512, S)
    blk = pl.BlockSpec((tm, D_MODEL), lambda i: (i, 0))
    return pl.pallas_call(
        _out_kernel,
        grid=(S // tm,),
        in_specs=[blk,
                  pl.BlockSpec((D_MODEL, D_MODEL), lambda i: (0, 0)),
                  blk,
                  pl.BlockSpec((1, D_MODEL), lambda i: (0, 0))],
        out_specs=blk,
        out_shape=jax.ShapeDtypeStruct((S, D_MODEL), F32),
        compiler_params=_cparams(("parallel",)),
        name="out_proj",
    )(m, w_out, x, g.reshape(1, D_MODEL))


def _layer_weights(w_in, lru_gate_w, rwkv_w_up, rwkv_a_up):
    o = 0
    xa_w, ga_w = w_in[:, o:o + D_BR], w_in[:, o + D_BR:o + 2 * D_BR]
    o += 2 * D_BR
    qkv_w = w_in[:, o:o + 3 * D_BR]
    fl_w = w_in[:, o + 3 * D_BR:o + 3 * D_BR + H_B]
    gb_w = w_in[:, o + 3 * D_BR + H_B:o + 4 * D_BR + H_B]
    o += 4 * D_BR + H_B
    pc_w = w_in[:, o:o + 3 * D_BR + 2 * LORA]
    o += 3 * D_BR + 2 * LORA
    gc_w = w_in[:, o:o + D_BR]
    o += D_BR
    mg_w = w_in[:, o:]
    pad = jnp.zeros((D_MODEL, P_COLS - (PC_FL * LANE + H_B)), BF16)
    w_p = jnp.concatenate([xa_w, ga_w, gb_w, gc_w, pc_w, fl_w, pad], axis=1)
    eye = jnp.eye(H_A, dtype=F32)
    wg = jnp.einsum('ghij,hk->ghikj', lru_gate_w, eye).reshape(2, D_BR, D_BR)
    wg = jnp.concatenate([wg[0], wg[1]], axis=1).astype(BF16)
    z = jnp.zeros((LORA, D_BR), F32)
    wup = jnp.concatenate([rwkv_w_up, z], axis=0).astype(BF16)
    aup = jnp.concatenate([z, rwkv_a_up], axis=0).astype(BF16)
    return w_p, qkv_w, mg_w, wg, wup, aup


def kernel(x, pre_norm_w, post_norm_w, w_in, b_merge, conv_w, conv_b, lru_gate_w, lru_gate_b, lru_lambda,
           fox_b_f, rwkv_mu, rwkv_w0, rwkv_w_up, rwkv_a0, rwkv_a_up, rwkv_k_k, rwkv_k_a, rwkv_r_k,
           rwkv_ln_w, rwkv_ln_b, w_branch, w_out):
    B, S, _ = x.shape
    assert B == 1 and S % CHUNK == 0
    depth = w_in.shape[0]
    w_in_b = w_in.astype(BF16)
    xs = x.reshape(S, D_MODEL)
    ones_p = jnp.ones((1, P_COLS), F32)
    qscale = jnp.concatenate([jnp.full((1, D_BR), DH_B ** -0.5 * LOG2E, F32), jnp.ones((1, 2 * D_BR), F32)], axis=1)
    row = lambda a: a.reshape(1, -1)
    for l in range(depth):
        w_p, w_qkv, w_mg, wg, wup, aup = _layer_weights(w_in_b[l], lru_gate_w[l], rwkv_w_up[l], rwkv_a_up[l])
        h = _rmsnorm_bf16(xs, pre_norm_w[l])
        P = _proj(h, w_p, ones_p, F32, 1536)
        qkv = _proj(h, w_qkv, qscale, BF16, 1024)
        ya = _rglru(P, conv_w[l], row(conv_b[l]), wg, row(lru_gate_b[l]), row(lru_lambda[l]))
        bf_pad = jnp.concatenate([fox_b_f[l], jnp.zeros((LANE - H_B,), F32)]).reshape(1, LANE)
        kx, cs = _fox_c(P, bf_pad)
        yb = _fox(qkv, kx, cs, P)
        mu = rwkv_mu[l]
        mu3 = mu[:3 * D_BR].reshape(3, D_BR)
        muwa = mu[3 * D_BR:].reshape(1, LANE)
        mt, nt, rh, yl, bonus = _rwkv_chunks(P, mu3, muwa, row(rwkv_w0[l]), wup, row(rwkv_a0[l]), aup,
                                             row(rwkv_k_k[l]), row(rwkv_k_a[l]), row(rwkv_r_k[l]))
        yc = _rwkv_seq(mt, nt, rh, yl, bonus, P, row(rwkv_ln_w[l]), row(rwkv_ln_b[l]))
        m = _merge(h, w_mg, b_merge[l], ya, yb, yc, w_branch[l].astype(BF16))
        xs = _out_proj(m, w_out[l].astype(BF16), xs, post_norm_w[l])
    return xs.reshape(B, S, D_MODEL)
```

```python
import functools

import jax
import jax.numpy as jnp
from jax import lax
from jax.experimental import pallas as pl
from jax.experimental.pallas import tpu as pltpu

F32 = jnp.float32
BF16 = jnp.bfloat16

D_MODEL = 2048
D_BR = 1024
H_A, BW_A = 16, 64
CONV_W = 4
LRU_C = 8.0
H_B, DH_B = 8, 128
H_C, N_C = 16, 64
LORA = 64
NORM_EPS = 1e-6
GN_EPS = 64e-5
LANE = 128
SUB = 8
CHUNK = 64
N_PAIR = H_C // 2
RW_NCH = 2
NEG = -1e30
LOG2E = 1.4426950408889634
FOX_Q = 1024
FOX_T = 1024
FOX_NB = 2
FOX_VMEM = 56 * 1024 * 1024

PC_XA, PC_GA, PC_GB, PC_GC, PC_CR, PC_CK, PC_CV = 0, 8, 16, 24, 32, 40, 48
PC_WA, PC_FL = 56, 57
P_COLS = 60 * LANE

VMEM_LIMIT = 48 * 1024 * 1024


def _cparams(sem):
    return pltpu.CompilerParams(dimension_semantics=sem, vmem_limit_bytes=VMEM_LIMIT)


def _sigmoid(x):
    return 0.5 * jnp.tanh(0.5 * x) + 0.5


def _silu(x):
    return x * _sigmoid(x)


def _softplus(z):
    return jnp.maximum(z, 0.0) + jnp.log1p(jnp.exp(-jnp.abs(z)))


def _split3(x):
    hi = x.astype(BF16)
    r1 = x - hi.astype(F32)
    mid = r1.astype(BF16)
    return hi, mid, (r1 - mid.astype(F32)).astype(BF16)


def _tri_cumsum(lower_mask, x):
    tri = lower_mask.astype(BF16)
    hi, mid, lo = _split3(x)
    return (jnp.dot(tri, hi, preferred_element_type=F32) + jnp.dot(tri, mid, preferred_element_type=F32)
            + jnp.dot(tri, lo, preferred_element_type=F32))


def _norm_kernel(x_ref, g_ref, o_ref):
    x = x_ref[...]
    y = x * lax.rsqrt(jnp.mean(x * x, axis=-1, keepdims=True) + NORM_EPS)
    o_ref[...] = (y * g_ref[...]).astype(o_ref.dtype)


def _rmsnorm_bf16(x, g):
    S = x.shape[0]
    tm = min(512, S)
    return pl.pallas_call(
        _norm_kernel,
        grid=(S // tm,),
        in_specs=[pl.BlockSpec((tm, D_MODEL), lambda i: (i, 0)),
                  pl.BlockSpec((1, D_MODEL), lambda i: (0, 0))],
        out_specs=pl.BlockSpec((tm, D_MODEL), lambda i: (i, 0)),
        out_shape=jax.ShapeDtypeStruct((S, D_MODEL), BF16),
        compiler_params=_cparams(("parallel",)),
        name="rmsnorm",
    )(x, g.reshape(1, D_MODEL))


def _proj_kernel(h_ref, w_ref, s_ref, o_ref):
    acc = jnp.dot(h_ref[...], w_ref[...], preferred_element_type=F32)
    o_ref[...] = (acc * s_ref[...]).astype(o_ref.dtype)


def _proj(h, w_all, layer, col0, colscale, out_dtype, tn):
    S, K = h.shape
    N = colscale.shape[1]
    tm = min(1024, S)
    j0 = col0 // tn
    assert col0 % tn == 0 and N % tn == 0
    return pl.pallas_call(
        _proj_kernel,
        grid=(S // tm, N // tn),
        in_specs=[pl.BlockSpec((tm, K), lambda i, j: (i, 0)),
                  pl.BlockSpec((None, K, tn), lambda i, j: (layer, 0, j0 + j)),
                  pl.BlockSpec((1, tn), lambda i, j: (0, j))],
        out_specs=pl.BlockSpec((tm, tn), lambda i, j: (i, j)),
        out_shape=jax.ShapeDtypeStruct((S, N), out_dtype),
        compiler_params=_cparams(("parallel", "parallel")),
        name="in_proj",
    )(h, w_all, colscale)


def _rglru_kernel(xa_ref, ga_ref, cw_ref, cb_ref, wg_ref, gbias_ref, lam_ref, o_ref,
                  xe_ref, hc_ref, a_ref, b_ref, *, T):
    i = pl.program_id(0)

    @pl.when(i == 0)
    def _():
        xe_ref[...] = jnp.zeros_like(xe_ref)
        hc_ref[...] = jnp.zeros_like(hc_ref)

    G = T // SUB
    sub = lax.broadcasted_iota(jnp.int32, (G, SUB, D_BR), 1)

    def rot(v, d):
        return pltpu.roll(v, d, 1)

    x = xa_ref[...].reshape(G, SUB, D_BR)
    x_prev = jnp.concatenate([xe_ref[...].reshape(1, SUB, D_BR), x[:G - 1]], axis=0)
    xe_ref[...] = xa_ref[T - SUB:T, :]
    cw = cw_ref[...]
    xc = cb_ref[...] + x * cw[3:4]
    for d in range(1, CONV_W):
        xc = xc + jnp.where(sub < d, rot(x_prev, d), rot(x, d)) * cw[3 - d:4 - d]
    xc = xc.reshape(T, D_BR)

    gates = jnp.dot(xc.astype(BF16), wg_ref[...], preferred_element_type=F32) + gbias_ref[...]
    r = _sigmoid(gates[:, :D_BR])
    ig = _sigmoid(gates[:, D_BR:])
    log_a = (-LRU_C) * r * _softplus(-lam_ref[...])
    a = jnp.exp(log_a)
    y = -jnp.tanh(log_a) * (1.0 + a * a)
    mult = jnp.where(y > 0.0, y * lax.rsqrt(y), 0.0)
    rows = lax.broadcasted_iota(jnp.int32, (T, D_BR), 0)
    mult = jnp.where((rows + i * T) == 0, 1.0, mult)
    b = (mult * ig * xc).reshape(G, SUB, D_BR)
    a = a.reshape(G, SUB, D_BR)

    for d in (1, 2, 4):
        keep = sub >= d
        b = jnp.where(keep, a * rot(b, d) + b, b)
        a = jnp.where(keep, a * rot(a, d), a)
    a_ref[...] = a.reshape(T, D_BR)
    b_ref[...] = b.reshape(T, D_BR)
    carry = hc_ref[...]
    for g in range(T // SUB):
        sl = slice(g * SUB, (g + 1) * SUB)
        hg = b_ref[sl, :] + a_ref[sl, :] * carry
        b_ref[sl, :] = hg
        carry = jnp.broadcast_to(hg[SUB - 1:SUB, :], (SUB, D_BR))
    hc_ref[...] = carry
    o_ref[...] = (b_ref[...] * _silu(ga_ref[...])).astype(o_ref.dtype)


def _rglru(P, conv_w, conv_b, wg, gbias, lam):
    S = P.shape[0]
    T = min(256, S)
    nb = D_BR // LANE
    return pl.pallas_call(
        functools.partial(_rglru_kernel, T=T),
        grid=(S // T,),
        in_specs=[pl.BlockSpec((T, D_BR), lambda i: (i, PC_XA // nb)),
                  pl.BlockSpec((T, D_BR), lambda i: (i, PC_GA // nb)),
                  pl.BlockSpec((CONV_W, D_BR), lambda i: (0, 0)),
                  pl.BlockSpec((1, D_BR), lambda i: (0, 0)),
                  pl.BlockSpec((D_BR, 2 * D_BR), lambda i: (0, 0)),
                  pl.BlockSpec((1, 2 * D_BR), lambda i: (0, 0)),
                  pl.BlockSpec((1, D_BR), lambda i: (0, 0))],
        out_specs=pl.BlockSpec((T, D_BR), lambda i: (i, 0)),
        out_shape=jax.ShapeDtypeStruct((S, D_BR), BF16),
        scratch_shapes=[pltpu.VMEM((SUB, D_BR), F32),
                        pltpu.VMEM((SUB, D_BR), F32),
                        pltpu.VMEM((T, D_BR), F32),
                        pltpu.VMEM((T, D_BR), F32)],
        compiler_params=_cparams(("arbitrary",)),
        name="rglru",
    )(P, P, conv_w, conv_b, wg, gbias, lam)


def _fox_c_kernel(fl_ref, bf_ref, kx_ref, cs_ref, carry_ref, *, T):
    i = pl.program_id(0)

    @pl.when(i == 0)
    def _():
        carry_ref[...] = jnp.zeros_like(carry_ref)

    z = fl_ref[...] + bf_ref[...]
    lf = (jnp.minimum(z, 0.0) - jnp.log1p(jnp.exp(-jnp.abs(z)))) * LOG2E
    TB = min(256, T)
    rr = lax.broadcasted_iota(jnp.int32, (TB, TB), 0)
    cc = lax.broadcasted_iota(jnp.int32, (TB, TB), 1)
    blocks, off = [], jnp.zeros((1, LANE), F32)
    for b in range(T // TB):
        blk = _tri_cumsum(cc <= rr, lf[b * TB:(b + 1) * TB]) + off
        blocks.append(blk)
        off = blk[TB - 1:TB, :]
    lc = jnp.concatenate(blocks, axis=0)
    carry = carry_ref[...]
    cs_ref[0] = carry
    carry_ref[...] = carry + jnp.broadcast_to(off, (SUB, LANE))
    lane = lax.broadcasted_iota(jnp.int32, (T, LANE), 1)
    hi, mid, lo = [jnp.where(lane < H_B, pc.astype(F32), 0.0) for pc in _split3(-lc)]
    packed = (hi + pltpu.roll(mid, H_B, 1) + pltpu.roll(lo, 2 * H_B, 1)).astype(BF16)
    rs = lax.broadcasted_iota(jnp.int32, (LANE, D_BR), 0)
    cs = lax.broadcasted_iota(jnp.int32, (LANE, D_BR), 1)
    sel = ((cs == (rs % H_B) * LANE + rs // H_B) & (rs < 3 * H_B)).astype(BF16)
    kx_ref[...] = jnp.dot(packed, sel, preferred_element_type=F32).astype(BF16)


def _fox_c(P, bf_pad):
    S = P.shape[0]
    T = min(FOX_T, FOX_Q, S // 2)
    return pl.pallas_call(
        functools.partial(_fox_c_kernel, T=T),
        grid=(S // T,),
        in_specs=[pl.BlockSpec((T, LANE), lambda i: (i, PC_FL)),
                  pl.BlockSpec((1, LANE), lambda i: (0, 0))],
        out_specs=[pl.BlockSpec((T, D_BR), lambda i: (i, 0)),
                   pl.BlockSpec((1, SUB, LANE), lambda i: (i, 0, 0))],
        out_shape=[jax.ShapeDtypeStruct((S, D_BR), BF16),
                   jax.ShapeDtypeStruct((S // T, SUB, LANE), F32)],
        scratch_shapes=[pltpu.VMEM((SUB, LANE), F32)],
        compiler_params=_cparams(("arbitrary",)),
        name="fox_cumgate",
    )(P, bf_pad)


def _fox_kernel(cs_ref, q_ref, k_ref, kx_ref, v_ref, gb_ref, o_ref, sh_ref, dp_ref, acc_ref, *, Q, T, NB):
    h = pl.program_id(0)
    i = pl.program_id(1)
    assert Q == T and NB == 2
    lane = lax.broadcasted_iota(jnp.int32, (Q, DH_B), 1)
    qa = jnp.concatenate([q_ref[...], jnp.where(lane < 3, 1.0, 0.0).astype(BF16)], axis=1)
    ones_v = jnp.ones((T, DH_B), BF16)
    c_i = cs_ref[h, i]

    def chunk(j):
        ks = pl.multiple_of(j * T, T)
        kc = jnp.concatenate([k_ref[pl.ds(ks, T), :], kx_ref[pl.ds(ks, T), :]], axis=1)
        vc = jnp.concatenate([v_ref[pl.ds(ks, T), :], ones_v], axis=1)
        return kc, vc, c_i - cs_ref[h, j]

    def qk(kc):
        return lax.dot_general(qa, kc, (((1,), (1,)), ((), ())), preferred_element_type=F32)

    def sm_pv(carry, s, vc, d):
        sh, d_prev, acc = carry
        a = sh + (d_prev - d)
        sh_new = jnp.maximum(a, jnp.max(s, axis=-1, keepdims=True))
        p = jnp.exp2(s - sh_new)
        acc = jnp.exp2(a - sh_new) * acc + jnp.dot(p.astype(BF16), vc, preferred_element_type=F32)
        return sh_new, d, acc

    def chunks(j0, n, carry, diag=False):
        cks = [chunk(j0 + c) for c in range(n)]
        s_next = qk(cks[0][0])
        for c in range(n):
            s = s_next
            if c + 1 < n:
                s_next = qk(cks[c + 1][0])
            elif diag:
                rows = lax.broadcasted_iota(jnp.int32, (Q, T), 0)
                cols = lax.broadcasted_iota(jnp.int32, (Q, T), 1)
                s = jnp.where(cols <= rows, s, NEG)
            carry = sm_pv(carry, s, cks[c][1], cks[c][2])
        return carry

    carry = (jnp.full((Q, 1), NEG, F32), jnp.float32(0.0), jnp.zeros((Q, 2 * DH_B), F32))
    n = i + 1
    tail = jnp.where(n % 2 == 0, 2, jnp.where(n == 1, 1, 3))
    sh, d_prev, acc = lax.fori_loop(0, (n - tail) // 2, lambda jj, c: chunks(2 * jj, 2, c), carry)
    sh_ref[...] = sh
    dp_ref[0] = d_prev
    acc_ref[...] = acc
    for t in (1, 2, 3):
        @pl.when(tail == t)
        def _():
            acc_ref[...] = chunks(n - t, t, (sh_ref[...], dp_ref[0], acc_ref[...]), diag=True)[2]
    acc = acc_ref[...]
    o_ref[...] = (acc[:, :DH_B] / acc[:, DH_B:] * _silu(gb_ref[...])).astype(o_ref.dtype)


def _fox(qkv, kx, cs, P):
    S = qkv.shape[0]
    Q = min(FOX_Q, S // 2)
    T = min(FOX_T, Q)
    cst = cs[:, 0, :H_B].T
    kv = lambda col0: pl.BlockSpec((S, DH_B), lambda h, i: (0, col0 + h))
    return pl.pallas_call(
        functools.partial(_fox_kernel, Q=Q, T=T, NB=FOX_NB),
        grid=(H_B, S // Q),
        in_specs=[pl.BlockSpec(memory_space=pltpu.SMEM),
                  pl.BlockSpec((Q, DH_B), lambda h, i: (i, h)),
                  kv(H_B), kv(0), kv(2 * H_B),
                  pl.BlockSpec((Q, DH_B), lambda h, i: (i, PC_GB + h))],
        out_specs=pl.BlockSpec((Q, DH_B), lambda h, i: (i, h)),
        out_shape=jax.ShapeDtypeStruct((S, D_BR), BF16),
        scratch_shapes=[pltpu.VMEM((Q, 1), F32), pltpu.SMEM((1,), F32),
                        pltpu.VMEM((Q, 2 * DH_B), F32)],
        compiler_params=pltpu.CompilerParams(dimension_semantics=("parallel", "parallel"),
                                             vmem_limit_bytes=FOX_VMEM),
        name="fox_attn",
    )(cst, qkv, qkv, kx, qkv, P)


def _head_sum(x):
    rr = lax.broadcasted_iota(jnp.int32, (LANE, LANE), 0) // N_C
    cc = lax.broadcasted_iota(jnp.int32, (LANE, LANE), 1) // N_C
    ones_bd = (rr == cc).astype(BF16)
    hi = x.astype(BF16)
    lo = (x - hi.astype(F32)).astype(BF16)
    outs = []
    for p in range(N_PAIR):
        sl = slice(p * LANE, (p + 1) * LANE)
        outs.append(jnp.dot(hi[:, sl], ones_bd, preferred_element_type=F32)
                    + jnp.dot(lo[:, sl], ones_bd, preferred_element_type=F32))
    return jnp.concatenate(outs, axis=1)


def _rwkv_tokens(cr_ref, ck_ref, cv_ref, wa_ref, pr_ref, pk_ref, pv_ref, pwa_ref,
                 mu_ref, muwa_ref, w0_ref, wup_ref, a0_ref, aup_ref, kk_ref, ka_ref):
    live = (pl.program_id(0) > 0).astype(F32)

    def shift_mix(cur_ref, prev_ref, mu):
        cur = cur_ref[...]
        rows = lax.broadcasted_iota(jnp.int32, cur.shape, 0)
        first = jnp.broadcast_to(prev_ref[SUB - 1:SUB, :] * live, cur.shape)
        prev = jnp.where(rows == 0, first, pltpu.roll(cur, 1, 0))
        return cur + (prev - cur) * mu

    r = shift_mix(cr_ref, pr_ref, mu_ref[0:1, :])
    kraw = shift_mix(ck_ref, pk_ref, mu_ref[1:2, :])
    v = shift_mix(cv_ref, pv_ref, mu_ref[2:3, :])
    wa = shift_mix(wa_ref, pwa_ref, muwa_ref[...])
    zw = w0_ref[...] + jnp.dot(jnp.tanh(wa).astype(BF16), wup_ref[...], preferred_element_type=F32)
    za = a0_ref[...] + jnp.dot(wa.astype(BF16), aup_ref[...], preferred_element_type=F32)
    g = (-0.6065306597126334) * _sigmoid(zw)
    a = _sigmoid(za)
    kk = kraw * kk_ref[...]
    ss = _head_sum(kk * kk)
    kkn = kk * lax.rsqrt(jnp.maximum(ss, 1e-24))
    return r, kraw * (1.0 + (a - 1.0) * ka_ref[...]), v, kkn, kkn * a, g


def _mm(a, b):
    return jnp.dot(a.astype(BF16), b.astype(BF16), preferred_element_type=F32)


def _mm_nt(a, b):
    return lax.dot_general(a.astype(BF16), b.astype(BF16), (((1,), (1,)), ((), ())),
                           preferred_element_type=F32)


def _mm_tn(a, b):
    return lax.dot_general(a.astype(BF16), b.astype(BF16), (((0,), (0,)), ((), ())),
                           preferred_element_type=F32)


def _rwkv_chunk_kernel(*refs, NCH):
    rk_ref, mt_ref, nt_ref, rh_ref, yl_ref, bonus_ref = refs[16:]
    L = CHUNK
    rv_a, kv_a, vm_a, kkn_a, bv_a, g_a = _rwkv_tokens(*refs[:16])
    bonus_ref[...] = (_head_sum(rv_a * kv_a * rk_ref[...]) * vm_a).astype(bonus_ref.dtype)
    rr = lax.broadcasted_iota(jnp.int32, (L, L), 0)
    cc = lax.broadcasted_iota(jnp.int32, (L, L), 1)
    ops = []
    for ch in range(NCH):
        rows = slice(ch * L, (ch + 1) * L)
        rv, kv, vm, bv, g = rv_a[rows], kv_a[rows], vm_a[rows], bv_a[rows], g_a[rows]
        G = _tri_cumsum(cc <= rr, g)
        Ge = G - g
        Gm = G[L // 2 - 1:L // 2, :]
        GL = G[L - 1:L, :]
        na = -kkn_a[rows]
        e_b = jnp.exp(Gm - G)
        e_l = jnp.exp(GL - G)
        ops.append(dict(al=na * jnp.exp(Ge), al_m=na * jnp.exp(Ge - Gm), be_m=bv * e_b, ka_m=kv * e_b,
                        rho=rv * jnp.exp(G), rho_m=rv * jnp.exp(G - Gm), b_hat=bv * e_l, k_hat=kv * e_l,
                        d_row=jnp.exp(GL), vm=vm))

    lo1 = lax.broadcasted_iota(jnp.int32, (L, LANE), 1) < N_C
    r4 = lax.broadcasted_iota(jnp.int32, (4 * L, LANE), 0)
    c4 = lax.broadcasted_iota(jnp.int32, (4 * L, LANE), 1) & (N_C - 1)
    tri4 = c4 < (r4 & (L - 1)) + jnp.where(r4 < 2 * L, 0, 1)
    rb = lax.broadcasted_iota(jnp.int32, (LANE, LANE), 0)
    cb = lax.broadcasted_iota(jnp.int32, (LANE, LANE), 1)
    bd = (rb // N_C) == (cb // N_C)
    eye = rb == cb
    zeros_l = jnp.zeros((L, LANE), BF16)

    units = [(ch, p) for ch in range(NCH) for p in range(N_PAIR)]
    nu = len(units)
    heads = [(n, hh) for n in range(nu) for hh in range(2)]
    sls = [slice(p * LANE, (p + 1) * LANE) for p in range(N_PAIR)]
    get = lambda name, u: ops[u[0]][name][:, sls[u[1]]]
    vms = [get('vm', u).astype(BF16) for u in units]
    a_cat = []
    for u in units:
        alm_p, rhm_p = get('al_m', u), get('rho_m', u)
        lhs4 = jnp.concatenate([jnp.where(lo1, alm_p, 0.0), jnp.where(lo1, 0.0, alm_p),
                                jnp.where(lo1, rhm_p, 0.0), jnp.where(lo1, 0.0, rhm_p)], axis=0)
        rhs2 = jnp.concatenate([get('be_m', u), get('ka_m', u)], axis=0)
        a_cat.append(jnp.where(tri4, _mm_nt(lhs4, rhs2), 0.0).astype(BF16))
    zv = [jnp.concatenate([zeros_l, v], axis=0) for v in vms]
    akv = [_mm(jnp.where(lo1, 0.0, a_cat[n][hh * L:(hh + 1) * L]), zv[n]) for n, hh in heads]
    xs, apow = [], []
    for n, u in enumerate(units):
        al_p = get('al', u)
        akv_p = jnp.where(lo1, akv[2 * n], akv[2 * n + 1])
        xs.append(jnp.where(lo1, al_p, pltpu.roll(akv_p, N_C, 1)))
        xs.append(jnp.where(lo1, pltpu.roll(al_p, N_C, 1), akv_p))
        for hh in range(2):
            apow.append(jnp.where(lo1, a_cat[n][hh * L:(hh + 1) * L], 0.0).astype(BF16))
    for step in range(6):
        if step < 5:
            res = [_mm(a[:, :N_C], jnp.concatenate([x.astype(BF16), a], axis=1)) for a, x in zip(apow, xs)]
            apow = [r[:, LANE:].astype(BF16) for r in res]
        else:
            res = [_mm(a[:, :N_C], x) for a, x in zip(apow, xs)]
        xs = [x + r[:, :LANE] for x, r in zip(xs, res)]
    w_pq = []
    for n in range(nu):
        x0, x1 = xs[2 * n], xs[2 * n + 1]
        p_pair = jnp.where(lo1, x0, pltpu.roll(x1, N_C, 1)).astype(BF16)
        q_pair = jnp.where(lo1, pltpu.roll(x0, N_C, 1), x1).astype(BF16)
        w_pq.append(jnp.concatenate([jnp.concatenate([p_pair, q_pair], axis=1),
                                     jnp.concatenate([zeros_l, vms[n]], axis=1)], axis=0))
    rbs = [_mm(a_cat[n][(2 + hh) * L:(3 + hh) * L], w_pq[n]) for n, hh in heads]
    gs = [_mm_tn(jnp.concatenate([get('b_hat', u), get('k_hat', u)], axis=0), w_pq[n])
          for n, u in enumerate(units)]
    for n, (ch, p) in enumerate(units):
        sl = sls[p]
        rows = slice(ch * L, (ch + 1) * L)
        r0, r1 = rbs[2 * n], rbs[2 * n + 1]
        rh_ref[rows, sl] = (get('rho', (ch, p)) + jnp.where(lo1, r0[:, :LANE], r1[:, :LANE])).astype(rh_ref.dtype)
        yl_ref[rows, sl] = jnp.where(lo1, r0[:, LANE:], r1[:, LANE:]).astype(yl_ref.dtype)
        d_p = jnp.broadcast_to(ops[ch]['d_row'][:, sl], (LANE, LANE))
        mt_ref[ch, p] = (jnp.where(eye, d_p, 0.0) + jnp.where(bd, gs[n][:, :LANE], 0.0)).astype(mt_ref.dtype)
        nt_ref[ch, p] = jnp.where(bd, gs[n][:, LANE:], 0.0)


def _rwkv_chunks(P, mu3, muwa, w0, wup, a0, aup, k_k, k_a, r_k):
    S = P.shape[0]
    nc = S // CHUNK
    nb = D_BR // LANE
    T = RW_NCH * CHUNK
    prev = lambda c: jnp.maximum(c * (T // SUB) - 1, 0)
    row = lambda n: pl.BlockSpec((1, n), lambda c: (0, 0))
    big = pl.BlockSpec((T, D_BR), lambda c: (c, 0))
    mat = pl.BlockSpec((RW_NCH, N_PAIR, LANE, LANE), lambda c: (c, 0, 0, 0))
    return pl.pallas_call(
        functools.partial(_rwkv_chunk_kernel, NCH=RW_NCH),
        grid=(nc // RW_NCH,),
        in_specs=[pl.BlockSpec((T, D_BR), lambda c: (c, PC_CR // nb)),
                  pl.BlockSpec((T, D_BR), lambda c: (c, PC_CK // nb)),
                  pl.BlockSpec((T, D_BR), lambda c: (c, PC_CV // nb)),
                  pl.BlockSpec((T, LANE), lambda c: (c, PC_WA)),
                  pl.BlockSpec((SUB, D_BR), lambda c: (prev(c), PC_CR // nb)),
                  pl.BlockSpec((SUB, D_BR), lambda c: (prev(c), PC_CK // nb)),
                  pl.BlockSpec((SUB, D_BR), lambda c: (prev(c), PC_CV // nb)),
                  pl.BlockSpec((SUB, LANE), lambda c: (prev(c), PC_WA)),
                  pl.BlockSpec((3, D_BR), lambda c: (0, 0)),
                  row(LANE), row(D_BR),
                  pl.BlockSpec((LANE, D_BR), lambda c: (0, 0)),
                  row(D_BR),
                  pl.BlockSpec((LANE, D_BR), lambda c: (0, 0)),
                  row(D_BR), row(D_BR), row(D_BR)],
        out_specs=[mat, mat, big, big, big],
        out_shape=[jax.ShapeDtypeStruct((nc, N_PAIR, LANE, LANE), BF16),
                   jax.ShapeDtypeStruct((nc, N_PAIR, LANE, LANE), F32),
                   jax.ShapeDtypeStruct((S, D_BR), BF16),
                   jax.ShapeDtypeStruct((S, D_BR), BF16),
                   jax.ShapeDtypeStruct((S, D_BR), BF16)],
        compiler_params=_cparams(("parallel",)),
        name="rwkv_chunks",
    )(P, P, P, P, P, P, P, P, mu3, muwa, w0, wup, a0, aup, k_k, k_a, r_k)


def _rwkv_seq_kernel(mt_ref, nt_ref, rh_ref, yl_ref, bonus_ref, gc_ref,
                     lnw_ref, lnb_ref, o_ref, z_ref, y_ref, *, NCH):
    i = pl.program_id(0)

    @pl.when(i == 0)
    def _():
        z_ref[...] = jnp.zeros_like(z_ref)

    zs = [z_ref[p] for p in range(N_PAIR)]
    for c in range(NCH):
        rows = slice(c * CHUNK, (c + 1) * CHUNK)
        zb = [z.astype(BF16) for z in zs]
        zs = [jnp.dot(mt_ref[c, p], zb[p], preferred_element_type=F32) + nt_ref[c, p]
              for p in range(N_PAIR)]
        for p in range(N_PAIR):
            sl = slice(p * LANE, (p + 1) * LANE)
            y_ref[rows, sl] = jnp.dot(rh_ref[rows, sl], zb[p], preferred_element_type=F32) + yl_ref[rows, sl]
    for p in range(N_PAIR):
        z_ref[p] = zs[p]

    y = y_ref[...]
    inv_n = 1.0 / N_C
    mu = _head_sum(y) * inv_n
    yc = y - mu
    var = _head_sum(yc * yc) * inv_n
    yn = yc * lax.rsqrt(var + GN_EPS) * lnw_ref[...] + lnb_ref[...]
    o_ref[...] = ((yn + bonus_ref[...]) * _silu(gc_ref[...])).astype(o_ref.dtype)


def _rwkv_seq(mt, nt, rh, yl, bonus, P, ln_w, ln_b):
    S = rh.shape[0]
    T = min(256, S)
    nch = T // CHUNK
    nb = D_BR // LANE
    big = pl.BlockSpec((T, D_BR), lambda i: (i, 0))
    mat = pl.BlockSpec((nch, N_PAIR, LANE, LANE), lambda i: (i, 0, 0, 0))
    row = pl.BlockSpec((1, D_BR), lambda i: (0, 0))
    return pl.pallas_call(
        functools.partial(_rwkv_seq_kernel, NCH=nch),
        grid=(S // T,),
        in_specs=[mat, mat, big, big, big,
                  pl.BlockSpec((T, D_BR), lambda i: (i, PC_GC // nb)),
                  row, row],
        out_specs=big,
        out_shape=jax.ShapeDtypeStruct((S, D_BR), BF16),
        scratch_shapes=[pltpu.VMEM((N_PAIR, LANE, LANE), F32),
                        pltpu.VMEM((T, D_BR), F32)],
        compiler_params=_cparams(("arbitrary",)),
        name="rwkv_seq",
    )(mt, nt, rh, yl, bonus, P, ln_w, ln_b)


def _merge_kernel(h_ref, wg0_ref, wg1_ref, wg2_ref, bm_ref, ya_ref, yb_ref, yc_ref,
                  wb0_ref, wb1_ref, wb2_ref, o_ref):
    h = h_ref[...]
    bm = bm_ref[...]
    acc = None
    for n, (wg_ref, y_ref, wb_ref) in enumerate(((wg0_ref, ya_ref, wb0_ref),
                                                  (wg1_ref, yb_ref, wb1_ref),
                                                  (wg2_ref, yc_ref, wb2_ref))):
        gate = _sigmoid(jnp.dot(h, wg_ref[...], preferred_element_type=F32) + bm[n:n + 1, :])
        term = gate * jnp.dot(y_ref[...], wb_ref[...], preferred_element_type=F32)
        acc = term if acc is None else acc + term
    o_ref[...] = acc.astype(o_ref.dtype)


def _merge(h, w_all, layer, b_merge, ya, yb, yc, w_br):
    S = h.shape[0]
    tm = min(1024, S)
    tn = 512
    nj = D_MODEL // tn
    ysp = pl.BlockSpec((tm, D_BR), lambda i, j: (i, 0))
    wgate = lambda n: pl.BlockSpec((None, D_MODEL, tn), lambda i, j: (layer, 0, n * nj + j))
    return pl.pallas_call(
        _merge_kernel,
        grid=(S // tm, nj),
        in_specs=[pl.BlockSpec((tm, D_MODEL), lambda i, j: (i, 0)),
                  wgate(0), wgate(1), wgate(2),
                  pl.BlockSpec((3, tn), lambda i, j: (0, j)),
                  ysp, ysp, ysp,
                  pl.BlockSpec((None, D_BR, tn), lambda i, j: (0, 0, j)),
                  pl.BlockSpec((None, D_BR, tn), lambda i, j: (1, 0, j)),
                  pl.BlockSpec((None, D_BR, tn), lambda i, j: (2, 0, j))],
        out_specs=pl.BlockSpec((tm, tn), lambda i, j: (i, j)),
        out_shape=jax.ShapeDtypeStruct((S, D_MODEL), BF16),
        compiler_params=pltpu.CompilerParams(dimension_semantics=("parallel", "parallel"),
                                             vmem_limit_bytes=FOX_VMEM),
        name="merge",
    )(h, w_all, w_all, w_all, b_merge, ya, yb, yc, w_br, w_br, w_br)


def _out_kernel(m_ref, w_ref, x_ref, g_ref, o_ref):
    y = jnp.dot(m_ref[...], w_ref[...], preferred_element_type=F32)
    yn = y * lax.rsqrt(jnp.mean(y * y, axis=-1, keepdims=True) + NORM_EPS)
    o_ref[...] = x_ref[...] + yn * g_ref[...]


def _out_proj(m, w_out, x, g):
    S = m.shape[0]
    tm = min(512, S)
    blk = pl.BlockSpec((tm, D_MODEL), lambda i: (i, 0))
    return pl.pallas_call(
        _out_kernel,
        grid=(S // tm,),
        in_specs=[blk,
                  pl.BlockSpec((D_MODEL, D_MODEL), lambda i: (0, 0)),
                  blk,
                  pl.BlockSpec((1, D_MODEL), lambda i: (0, 0))],
        out_specs=blk,
        out_shape=jax.ShapeDtypeStruct((S, D_MODEL), F32),
        compiler_params=_cparams(("parallel",)),
        name="out_proj",
    )(m, w_out, x, g.reshape(1, D_MODEL))


def _pack_w_in(w_in):
    o = 0
    xa_ga = w_in[..., o:o + 2 * D_BR]
    o += 2 * D_BR
    qkv_w = w_in[..., o:o + 3 * D_BR]
    fl_w = w_in[..., o + 3 * D_BR:o + 3 * D_BR + H_B]
    gb_w = w_in[..., o + 3 * D_BR + H_B:o + 4 * D_BR + H_B]
    o += 4 * D_BR + H_B
    pc_w = w_in[..., o:o + 3 * D_BR + 2 * LORA]
    o += 3 * D_BR + 2 * LORA
    gc_w = w_in[..., o:o + D_BR]
    o += D_BR
    mg_w = w_in[..., o:]
    pad = jnp.zeros(w_in.shape[:-1] + (P_COLS - (PC_FL * LANE + H_B),), w_in.dtype)
    return jnp.concatenate([mg_w, qkv_w, xa_ga, gb_w, gc_w, pc_w, fl_w, pad], axis=-1).astype(BF16)


W_COL_QKV = 3 * D_MODEL
W_COL_P = 3 * D_MODEL + 3 * D_BR


def _layer_weights(lru_gate_w, rwkv_w_up, rwkv_a_up):
    eye = jnp.eye(H_A, dtype=F32)
    wg = jnp.einsum('ghij,hk->ghikj', lru_gate_w, eye).reshape(2, D_BR, D_BR)
    wg = jnp.concatenate([wg[0], wg[1]], axis=1).astype(BF16)
    z = jnp.zeros((LORA, D_BR), F32)
    wup = jnp.concatenate([rwkv_w_up, z], axis=0).astype(BF16)
    aup = jnp.concatenate([z, rwkv_a_up], axis=0).astype(BF16)
    return wg, wup, aup


def kernel(x, pre_norm_w, post_norm_w, w_in, b_merge, conv_w, conv_b, lru_gate_w, lru_gate_b, lru_lambda,
           fox_b_f, rwkv_mu, rwkv_w0, rwkv_w_up, rwkv_a0, rwkv_a_up, rwkv_k_k, rwkv_k_a, rwkv_r_k,
           rwkv_ln_w, rwkv_ln_b, w_branch, w_out):
    B, S, _ = x.shape
    assert B == 1 and S % CHUNK == 0
    depth = w_in.shape[0]
    w_all = _pack_w_in(w_in)
    xs = x.reshape(S, D_MODEL)
    ones_p = jnp.ones((1, P_COLS), F32)
    qscale = jnp.concatenate([jnp.full((1, D_BR), DH_B ** -0.5 * LOG2E, F32), jnp.ones((1, 2 * D_BR), F32)], axis=1)
    row = lambda a: a.reshape(1, -1)
    for l in range(depth):
        wg, wup, aup = _layer_weights(lru_gate_w[l], rwkv_w_up[l], rwkv_a_up[l])
        h = _rmsnorm_bf16(xs, pre_norm_w[l])
        P = _proj(h, w_all, l, W_COL_P, ones_p, F32, 1536)
        qkv = _proj(h, w_all, l, W_COL_QKV, qscale, BF16, 1024)
        ya = _rglru(P, conv_w[l], row(conv_b[l]), wg, row(lru_gate_b[l]), row(lru_lambda[l]))
        bf_pad = jnp.concatenate([fox_b_f[l], jnp.zeros((LANE - H_B,), F32)]).reshape(1, LANE)
        kx, cs = _fox_c(P, bf_pad)
        yb = _fox(qkv, kx, cs, P)
        mu = rwkv_mu[l]
        mu3 = mu[:3 * D_BR].reshape(3, D_BR)
        muwa = mu[3 * D_BR:].reshape(1, LANE)
        mt, nt, rh, yl, bonus = _rwkv_chunks(P, mu3, muwa, row(rwkv_w0[l]), wup, row(rwkv_a0[l]), aup,
                                             row(rwkv_k_k[l]), row(rwkv_k_a[l]), row(rwkv_r_k[l]))
        yc = _rwkv_seq(mt, nt, rh, yl, bonus, P, row(rwkv_ln_w[l]), row(rwkv_ln_b[l]))
        m = _merge(h, w_all, l, b_merge[l], ya, yb, yc, w_branch[l].astype(BF16))
        xs = _out_proj(m, w_out[l].astype(BF16), xs, post_norm_w[l])
    return xs.reshape(B, S, D_MODEL)
```

```python
import functools

import jax
import jax.numpy as jnp
from jax import lax
from jax.experimental import pallas as pl
from jax.experimental.pallas import tpu as pltpu

F32 = jnp.float32
BF16 = jnp.bfloat16

D_MODEL = 2048
D_BR = 1024
H_A, BW_A = 16, 64
CONV_W = 4
LRU_C = 8.0
H_B, DH_B = 8, 128
H_C, N_C = 16, 64
LORA = 64
NORM_EPS = 1e-6
GN_EPS = 64e-5
LANE = 128
SUB = 8
CHUNK = 64
N_PAIR = H_C // 2
RW_NCH = 2
NEG = -1e30
LOG2E = 1.4426950408889634
FOX_Q = 1024
FOX_T = 1024
FOX_NB = 2
FOX_VMEM = 56 * 1024 * 1024

PC_XA, PC_GA, PC_GB, PC_GC, PC_CR, PC_CK, PC_CV = 0, 8, 16, 24, 32, 40, 48
PC_WA, PC_FL = 56, 57
P_COLS = 60 * LANE

VMEM_LIMIT = 48 * 1024 * 1024


def _cparams(sem):
    return pltpu.CompilerParams(dimension_semantics=sem, vmem_limit_bytes=VMEM_LIMIT)


def _sigmoid(x):
    return 0.5 * jnp.tanh(0.5 * x) + 0.5


def _silu(x):
    return x * _sigmoid(x)


def _softplus(z):
    return jnp.maximum(z, 0.0) + jnp.log1p(jnp.exp(-jnp.abs(z)))


def _split3(x):
    hi = x.astype(BF16)
    r1 = x - hi.astype(F32)
    mid = r1.astype(BF16)
    return hi, mid, (r1 - mid.astype(F32)).astype(BF16)


def _tri_cumsum(lower_mask, x):
    tri = lower_mask.astype(BF16)
    hi, mid, lo = _split3(x)
    return (jnp.dot(tri, hi, preferred_element_type=F32) + jnp.dot(tri, mid, preferred_element_type=F32)
            + jnp.dot(tri, lo, preferred_element_type=F32))


def _norm_kernel(x_ref, g_ref, o_ref):
    x = x_ref[...]
    y = x * lax.rsqrt(jnp.mean(x * x, axis=-1, keepdims=True) + NORM_EPS)
    o_ref[...] = (y * g_ref[...]).astype(o_ref.dtype)


def _rmsnorm_bf16(x, g):
    S = x.shape[0]
    tm = min(512, S)
    return pl.pallas_call(
        _norm_kernel,
        grid=(S // tm,),
        in_specs=[pl.BlockSpec((tm, D_MODEL), lambda i: (i, 0)),
                  pl.BlockSpec((1, D_MODEL), lambda i: (0, 0))],
        out_specs=pl.BlockSpec((tm, D_MODEL), lambda i: (i, 0)),
        out_shape=jax.ShapeDtypeStruct((S, D_MODEL), BF16),
        compiler_params=_cparams(("parallel",)),
        name="rmsnorm",
    )(x, g.reshape(1, D_MODEL))


def _proj_kernel(h_ref, w_ref, s_ref, o_ref):
    acc = jnp.dot(h_ref[...], w_ref[...], preferred_element_type=F32)
    o_ref[...] = (acc * s_ref[...]).astype(o_ref.dtype)


def _proj(h, w_all, layer, col0, colscale, out_dtype, tn):
    S, K = h.shape
    N = colscale.shape[1]
    tm = min(1024, S)
    j0 = col0 // tn
    assert col0 % tn == 0 and N % tn == 0
    return pl.pallas_call(
        _proj_kernel,
        grid=(S // tm, N // tn),
        in_specs=[pl.BlockSpec((tm, K), lambda i, j: (i, 0)),
                  pl.BlockSpec((None, K, tn), lambda i, j: (layer, 0, j0 + j)),
                  pl.BlockSpec((1, tn), lambda i, j: (0, j))],
        out_specs=pl.BlockSpec((tm, tn), lambda i, j: (i, j)),
        out_shape=jax.ShapeDtypeStruct((S, N), out_dtype),
        compiler_params=_cparams(("parallel", "parallel")),
        name="in_proj",
    )(h, w_all, colscale)


def _rglru_kernel(xa_ref, ga_ref, cw_ref, cb_ref, wg_ref, gbias_ref, lam_ref, o_ref,
                  xe_ref, hc_ref, a_ref, b_ref, *, T):
    i = pl.program_id(0)

    @pl.when(i == 0)
    def _():
        xe_ref[...] = jnp.zeros_like(xe_ref)
        hc_ref[...] = jnp.zeros_like(hc_ref)

    G = T // SUB
    sub = lax.broadcasted_iota(jnp.int32, (G, SUB, D_BR), 1)

    def rot(v, d):
        return pltpu.roll(v, d, 1)

    x = xa_ref[...].reshape(G, SUB, D_BR)
    x_prev = jnp.concatenate([xe_ref[...].reshape(1, SUB, D_BR), x[:G - 1]], axis=0)
    xe_ref[...] = xa_ref[T - SUB:T, :]
    cw = cw_ref[...]
    xc = cb_ref[...] + x * cw[3:4]
    for d in range(1, CONV_W):
        xc = xc + jnp.where(sub < d, rot(x_prev, d), rot(x, d)) * cw[3 - d:4 - d]
    xc = xc.reshape(T, D_BR)

    gates = jnp.dot(xc.astype(BF16), wg_ref[...], preferred_element_type=F32) + gbias_ref[...]
    r = _sigmoid(gates[:, :D_BR])
    ig = _sigmoid(gates[:, D_BR:])
    log_a = (-LRU_C) * r * _softplus(-lam_ref[...])
    a = jnp.exp(log_a)
    y = -jnp.tanh(log_a) * (1.0 + a * a)
    mult = jnp.where(y > 0.0, y * lax.rsqrt(y), 0.0)
    rows = lax.broadcasted_iota(jnp.int32, (T, D_BR), 0)
    mult = jnp.where((rows + i * T) == 0, 1.0, mult)
    b = (mult * ig * xc).reshape(G, SUB, D_BR)
    a = a.reshape(G, SUB, D_BR)

    for d in (1, 2, 4):
        keep = sub >= d
        b = jnp.where(keep, a * rot(b, d) + b, b)
        a = jnp.where(keep, a * rot(a, d), a)
    a_ref[...] = a.reshape(T, D_BR)
    b_ref[...] = b.reshape(T, D_BR)
    carry = hc_ref[...]
    for g in range(T // SUB):
        sl = slice(g * SUB, (g + 1) * SUB)
        hg = b_ref[sl, :] + a_ref[sl, :] * carry
        b_ref[sl, :] = hg
        carry = jnp.broadcast_to(hg[SUB - 1:SUB, :], (SUB, D_BR))
    hc_ref[...] = carry
    o_ref[...] = (b_ref[...] * _silu(ga_ref[...])).astype(o_ref.dtype)


def _rglru(P, conv_w, conv_b, wg, gbias, lam):
    S = P.shape[0]
    T = min(256, S)
    nb = D_BR // LANE
    return pl.pallas_call(
        functools.partial(_rglru_kernel, T=T),
        grid=(S // T,),
        in_specs=[pl.BlockSpec((T, D_BR), lambda i: (i, PC_XA // nb)),
                  pl.BlockSpec((T, D_BR), lambda i: (i, PC_GA // nb)),
                  pl.BlockSpec((CONV_W, D_BR), lambda i: (0, 0)),
                  pl.BlockSpec((1, D_BR), lambda i: (0, 0)),
                  pl.BlockSpec((D_BR, 2 * D_BR), lambda i: (0, 0)),
                  pl.BlockSpec((1, 2 * D_BR), lambda i: (0, 0)),
                  pl.BlockSpec((1, D_BR), lambda i: (0, 0))],
        out_specs=pl.BlockSpec((T, D_BR), lambda i: (i, 0)),
        out_shape=jax.ShapeDtypeStruct((S, D_BR), BF16),
        scratch_shapes=[pltpu.VMEM((SUB, D_BR), F32),
                        pltpu.VMEM((SUB, D_BR), F32),
                        pltpu.VMEM((T, D_BR), F32),
                        pltpu.VMEM((T, D_BR), F32)],
        compiler_params=_cparams(("arbitrary",)),
        name="rglru",
    )(P, P, conv_w, conv_b, wg, gbias, lam)


def _fox_c_kernel(fl_ref, bf_ref, kx_ref, cs_ref, carry_ref, *, T):
    i = pl.program_id(0)

    @pl.when(i == 0)
    def _():
        carry_ref[...] = jnp.zeros_like(carry_ref)

    z = fl_ref[...] + bf_ref[...]
    lf = (jnp.minimum(z, 0.0) - jnp.log1p(jnp.exp(-jnp.abs(z)))) * LOG2E
    TB = min(256, T)
    rr = lax.broadcasted_iota(jnp.int32, (TB, TB), 0)
    cc = lax.broadcasted_iota(jnp.int32, (TB, TB), 1)
    blocks, off = [], jnp.zeros((1, LANE), F32)
    for b in range(T // TB):
        blk = _tri_cumsum(cc <= rr, lf[b * TB:(b + 1) * TB]) + off
        blocks.append(blk)
        off = blk[TB - 1:TB, :]
    lc = jnp.concatenate(blocks, axis=0)
    carry = carry_ref[...]
    cs_ref[0] = carry
    carry_ref[...] = carry + jnp.broadcast_to(off, (SUB, LANE))
    lane = lax.broadcasted_iota(jnp.int32, (T, LANE), 1)
    hi, mid, lo = [jnp.where(lane < H_B, pc.astype(F32), 0.0) for pc in _split3(-lc)]
    packed = (hi + pltpu.roll(mid, H_B, 1) + pltpu.roll(lo, 2 * H_B, 1)).astype(BF16)
    rs = lax.broadcasted_iota(jnp.int32, (LANE, D_BR), 0)
    cs = lax.broadcasted_iota(jnp.int32, (LANE, D_BR), 1)
    sel = ((cs == (rs % H_B) * LANE + rs // H_B) & (rs < 3 * H_B)).astype(BF16)
    kx_ref[...] = jnp.dot(packed, sel, preferred_element_type=F32).astype(BF16)


def _fox_c(P, bf_pad):
    S = P.shape[0]
    T = min(FOX_T, FOX_Q, S // 2)
    return pl.pallas_call(
        functools.partial(_fox_c_kernel, T=T),
        grid=(S // T,),
        in_specs=[pl.BlockSpec((T, LANE), lambda i: (i, PC_FL)),
                  pl.BlockSpec((1, LANE), lambda i: (0, 0))],
        out_specs=[pl.BlockSpec((T, D_BR), lambda i: (i, 0)),
                   pl.BlockSpec((1, SUB, LANE), lambda i: (i, 0, 0))],
        out_shape=[jax.ShapeDtypeStruct((S, D_BR), BF16),
                   jax.ShapeDtypeStruct((S // T, SUB, LANE), F32)],
        scratch_shapes=[pltpu.VMEM((SUB, LANE), F32)],
        compiler_params=_cparams(("arbitrary",)),
        name="fox_cumgate",
    )(P, bf_pad)


def _fox_kernel(cs_ref, q_ref, k_ref, kx_ref, v_ref, gb_ref, o_ref, sh_ref, dp_ref, acc_ref, *, Q, T, NB):
    h = pl.program_id(0)
    i = pl.program_id(1)
    assert Q == T and NB == 2
    lane = lax.broadcasted_iota(jnp.int32, (Q, DH_B), 1)
    qa = jnp.concatenate([q_ref[...], jnp.where(lane < 3, 1.0, 0.0).astype(BF16)], axis=1)
    ones_v = jnp.ones((T, DH_B), BF16)
    c_i = cs_ref[h, i]

    def chunk(j):
        ks = pl.multiple_of(j * T, T)
        kc = jnp.concatenate([k_ref[pl.ds(ks, T), :], kx_ref[pl.ds(ks, T), :]], axis=1)
        vc = jnp.concatenate([v_ref[pl.ds(ks, T), :], ones_v], axis=1)
        return kc, vc, c_i - cs_ref[h, j]

    def qk(kc):
        return lax.dot_general(qa, kc, (((1,), (1,)), ((), ())), preferred_element_type=F32)

    def sm_pv(carry, s, vc, d):
        sh, d_prev, acc = carry
        a = sh + (d_prev - d)
        sh_new = jnp.maximum(a, jnp.max(s, axis=-1, keepdims=True))
        p = jnp.exp2(s - sh_new)
        acc = jnp.exp2(a - sh_new) * acc + jnp.dot(p.astype(BF16), vc, preferred_element_type=F32)
        return sh_new, d, acc

    def chunks(j0, n, carry, diag=False):
        cks = [chunk(j0 + c) for c in range(n)]
        s_next = qk(cks[0][0])
        for c in range(n):
            s = s_next
            if c + 1 < n:
                s_next = qk(cks[c + 1][0])
            elif diag:
                rows = lax.broadcasted_iota(jnp.int32, (Q, T), 0)
                cols = lax.broadcasted_iota(jnp.int32, (Q, T), 1)
                s = jnp.where(cols <= rows, s, NEG)
            carry = sm_pv(carry, s, cks[c][1], cks[c][2])
        return carry

    carry = (jnp.full((Q, 1), NEG, F32), jnp.float32(0.0), jnp.zeros((Q, 2 * DH_B), F32))
    n = i + 1
    tail = jnp.where(n % 2 == 0, 2, jnp.where(n == 1, 1, 3))
    sh, d_prev, acc = lax.fori_loop(0, (n - tail) // 2, lambda jj, c: chunks(2 * jj, 2, c), carry)
    sh_ref[...] = sh
    dp_ref[0] = d_prev
    acc_ref[...] = acc
    for t in (1, 2, 3):
        @pl.when(tail == t)
        def _():
            acc_ref[...] = chunks(n - t, t, (sh_ref[...], dp_ref[0], acc_ref[...]), diag=True)[2]
    acc = acc_ref[...]
    o_ref[...] = (acc[:, :DH_B] / acc[:, DH_B:] * _silu(gb_ref[...])).astype(o_ref.dtype)


def _fox(qkv, kx, cs, P):
    S = qkv.shape[0]
    Q = min(FOX_Q, S // 2)
    T = min(FOX_T, Q)
    cst = cs[:, 0, :H_B].T
    kv = lambda col0: pl.BlockSpec((S, DH_B), lambda h, i: (0, col0 + h))
    return pl.pallas_call(
        functools.partial(_fox_kernel, Q=Q, T=T, NB=FOX_NB),
        grid=(H_B, S // Q),
        in_specs=[pl.BlockSpec(memory_space=pltpu.SMEM),
                  pl.BlockSpec((Q, DH_B), lambda h, i: (i, h)),
                  kv(H_B), kv(0), kv(2 * H_B),
                  pl.BlockSpec((Q, DH_B), lambda h, i: (i, PC_GB + h))],
        out_specs=pl.BlockSpec((Q, DH_B), lambda h, i: (i, h)),
        out_shape=jax.ShapeDtypeStruct((S, D_BR), BF16),
        scratch_shapes=[pltpu.VMEM((Q, 1), F32), pltpu.SMEM((1,), F32),
                        pltpu.VMEM((Q, 2 * DH_B), F32)],
        compiler_params=pltpu.CompilerParams(dimension_semantics=("parallel", "parallel"),
                                             vmem_limit_bytes=FOX_VMEM),
        name="fox_attn",
    )(cst, qkv, qkv, kx, qkv, P)


def _head_sum(x):
    rr = lax.broadcasted_iota(jnp.int32, (LANE, LANE), 0) // N_C
    cc = lax.broadcasted_iota(jnp.int32, (LANE, LANE), 1) // N_C
    ones_bd = (rr == cc).astype(BF16)
    hi = x.astype(BF16)
    lo = (x - hi.astype(F32)).astype(BF16)
    outs = []
    for p in range(N_PAIR):
        sl = slice(p * LANE, (p + 1) * LANE)
        outs.append(jnp.dot(hi[:, sl], ones_bd, preferred_element_type=F32)
                    + jnp.dot(lo[:, sl], ones_bd, preferred_element_type=F32))
    return jnp.concatenate(outs, axis=1)


def _rwkv_tokens(cr_ref, ck_ref, cv_ref, wa_ref, pr_ref, pk_ref, pv_ref, pwa_ref,
                 mu_ref, muwa_ref, w0_ref, wup_ref, a0_ref, aup_ref, kk_ref, ka_ref):
    live = (pl.program_id(0) > 0).astype(F32)

    def shift_mix(cur_ref, prev_ref, mu):
        cur = cur_ref[...]
        rows = lax.broadcasted_iota(jnp.int32, cur.shape, 0)
        first = jnp.broadcast_to(prev_ref[SUB - 1:SUB, :] * live, cur.shape)
        prev = jnp.where(rows == 0, first, pltpu.roll(cur, 1, 0))
        return cur + (prev - cur) * mu

    r = shift_mix(cr_ref, pr_ref, mu_ref[0:1, :])
    kraw = shift_mix(ck_ref, pk_ref, mu_ref[1:2, :])
    v = shift_mix(cv_ref, pv_ref, mu_ref[2:3, :])
    wa = shift_mix(wa_ref, pwa_ref, muwa_ref[...])
    zw = w0_ref[...] + jnp.dot(jnp.tanh(wa).astype(BF16), wup_ref[...], preferred_element_type=F32)
    za = a0_ref[...] + jnp.dot(wa.astype(BF16), aup_ref[...], preferred_element_type=F32)
    g = (-0.6065306597126334) * _sigmoid(zw)
    a = _sigmoid(za)
    kk = kraw * kk_ref[...]
    ss = _head_sum(kk * kk)
    kkn = kk * lax.rsqrt(jnp.maximum(ss, 1e-24))
    return r, kraw * (1.0 + (a - 1.0) * ka_ref[...]), v, kkn, kkn * a, g


def _mm(a, b):
    return jnp.dot(a.astype(BF16), b.astype(BF16), preferred_element_type=F32)


def _mm_nt(a, b):
    return lax.dot_general(a.astype(BF16), b.astype(BF16), (((1,), (1,)), ((), ())),
                           preferred_element_type=F32)


def _mm_tn(a, b):
    return lax.dot_general(a.astype(BF16), b.astype(BF16), (((0,), (0,)), ((), ())),
                           preferred_element_type=F32)


def _rwkv_chunk_kernel(*refs, NCH):
    rk_ref, mt_ref, nt_ref, rh_ref, yl_ref, bonus_ref = refs[16:]
    L = CHUNK
    rv_a, kv_a, vm_a, kkn_a, bv_a, g_a = _rwkv_tokens(*refs[:16])
    bonus_ref[...] = (_head_sum(rv_a * kv_a * rk_ref[...]) * vm_a).astype(bonus_ref.dtype)
    rr = lax.broadcasted_iota(jnp.int32, (L, L), 0)
    cc = lax.broadcasted_iota(jnp.int32, (L, L), 1)
    ops = []
    for ch in range(NCH):
        rows = slice(ch * L, (ch + 1) * L)
        rv, kv, vm, bv, g = rv_a[rows], kv_a[rows], vm_a[rows], bv_a[rows], g_a[rows]
        G = _tri_cumsum(cc <= rr, g)
        Ge = G - g
        Gm = G[L // 2 - 1:L // 2, :]
        GL = G[L - 1:L, :]
        na = -kkn_a[rows]
        e_b = jnp.exp(Gm - G)
        e_l = jnp.exp(GL - G)
        ops.append(dict(al=na * jnp.exp(Ge), al_m=na * jnp.exp(Ge - Gm), be_m=bv * e_b, ka_m=kv * e_b,
                        rho=rv * jnp.exp(G), rho_m=rv * jnp.exp(G - Gm), b_hat=bv * e_l, k_hat=kv * e_l,
                        d_row=jnp.exp(GL), vm=vm))

    lo1 = lax.broadcasted_iota(jnp.int32, (L, LANE), 1) < N_C
    r4 = lax.broadcasted_iota(jnp.int32, (4 * L, LANE), 0)
    c4 = lax.broadcasted_iota(jnp.int32, (4 * L, LANE), 1) & (N_C - 1)
    tri4 = c4 < (r4 & (L - 1)) + jnp.where(r4 < 2 * L, 0, 1)
    rb = lax.broadcasted_iota(jnp.int32, (LANE, LANE), 0)
    cb = lax.broadcasted_iota(jnp.int32, (LANE, LANE), 1)
    bd = (rb // N_C) == (cb // N_C)
    eye = rb == cb
    zeros_l = jnp.zeros((L, LANE), BF16)

    units = [(ch, p) for ch in range(NCH) for p in range(N_PAIR)]
    nu = len(units)
    heads = [(n, hh) for n in range(nu) for hh in range(2)]
    sls = [slice(p * LANE, (p + 1) * LANE) for p in range(N_PAIR)]
    get = lambda name, u: ops[u[0]][name][:, sls[u[1]]]
    vms = [get('vm', u).astype(BF16) for u in units]
    a_cat = []
    for u in units:
        alm_p, rhm_p = get('al_m', u), get('rho_m', u)
        lhs4 = jnp.concatenate([jnp.where(lo1, alm_p, 0.0), jnp.where(lo1, 0.0, alm_p),
                                jnp.where(lo1, rhm_p, 0.0), jnp.where(lo1, 0.0, rhm_p)], axis=0)
        rhs2 = jnp.concatenate([get('be_m', u), get('ka_m', u)], axis=0)
        a_cat.append(jnp.where(tri4, _mm_nt(lhs4, rhs2), 0.0).astype(BF16))
    zv = [jnp.concatenate([zeros_l, v], axis=0) for v in vms]
    akv = [_mm(jnp.where(lo1, 0.0, a_cat[n][hh * L:(hh + 1) * L]), zv[n]) for n, hh in heads]
    xs, apow = [], []
    for n, u in enumerate(units):
        al_p = get('al', u)
        akv_p = jnp.where(lo1, akv[2 * n], akv[2 * n + 1])
        xs.append(jnp.where(lo1, al_p, pltpu.roll(akv_p, N_C, 1)))
        xs.append(jnp.where(lo1, pltpu.roll(al_p, N_C, 1), akv_p))
        for hh in range(2):
            apow.append(jnp.where(lo1, a_cat[n][hh * L:(hh + 1) * L], 0.0).astype(BF16))
    for step in range(6):
        if step < 5:
            res = [_mm(a[:, :N_C], jnp.concatenate([x.astype(BF16), a], axis=1)) for a, x in zip(apow, xs)]
            apow = [r[:, LANE:].astype(BF16) for r in res]
        else:
            res = [_mm(a[:, :N_C], x) for a, x in zip(apow, xs)]
        xs = [x + r[:, :LANE] for x, r in zip(xs, res)]
    w_pq = []
    for n in range(nu):
        x0, x1 = xs[2 * n], xs[2 * n + 1]
        p_pair = jnp.where(lo1, x0, pltpu.roll(x1, N_C, 1)).astype(BF16)
        q_pair = jnp.where(lo1, pltpu.roll(x0, N_C, 1), x1).astype(BF16)
        w_pq.append(jnp.concatenate([jnp.concatenate([p_pair, q_pair], axis=1),
                                     jnp.concatenate([zeros_l, vms[n]], axis=1)], axis=0))
    rbs = [_mm(a_cat[n][(2 + hh) * L:(3 + hh) * L], w_pq[n]) for n, hh in heads]
    gs = [_mm_tn(jnp.concatenate([get('b_hat', u), get('k_hat', u)], axis=0), w_pq[n])
          for n, u in enumerate(units)]
    for n, (ch, p) in enumerate(units):
        sl = sls[p]
        rows = slice(ch * L, (ch + 1) * L)
        r0, r1 = rbs[2 * n], rbs[2 * n + 1]
        rh_ref[rows, sl] = (get('rho', (ch, p)) + jnp.where(lo1, r0[:, :LANE], r1[:, :LANE])).astype(rh_ref.dtype)
        yl_ref[rows, sl] = jnp.where(lo1, r0[:, LANE:], r1[:, LANE:]).astype(yl_ref.dtype)
        d_p = jnp.broadcast_to(ops[ch]['d_row'][:, sl], (LANE, LANE))
        mt_ref[ch, p] = (jnp.where(eye, d_p, 0.0) + jnp.where(bd, gs[n][:, :LANE], 0.0)).astype(mt_ref.dtype)
        nt_ref[ch, p] = jnp.where(bd, gs[n][:, LANE:], 0.0)


def _rwkv_chunks(P, mu3, muwa, w0, wup, a0, aup, k_k, k_a, r_k):
    S = P.shape[0]
    nc = S // CHUNK
    nb = D_BR // LANE
    T = RW_NCH * CHUNK
    prev = lambda c: jnp.maximum(c * (T // SUB) - 1, 0)
    row = lambda n: pl.BlockSpec((1, n), lambda c: (0, 0))
    big = pl.BlockSpec((T, D_BR), lambda c: (c, 0))
    mat = pl.BlockSpec((RW_NCH, N_PAIR, LANE, LANE), lambda c: (c, 0, 0, 0))
    return pl.pallas_call(
        functools.partial(_rwkv_chunk_kernel, NCH=RW_NCH),
        grid=(nc // RW_NCH,),
        in_specs=[pl.BlockSpec((T, D_BR), lambda c: (c, PC_CR // nb)),
                  pl.BlockSpec((T, D_BR), lambda c: (c, PC_CK // nb)),
                  pl.BlockSpec((T, D_BR), lambda c: (c, PC_CV // nb)),
                  pl.BlockSpec((T, LANE), lambda c: (c, PC_WA)),
                  pl.BlockSpec((SUB, D_BR), lambda c: (prev(c), PC_CR // nb)),
                  pl.BlockSpec((SUB, D_BR), lambda c: (prev(c), PC_CK // nb)),
                  pl.BlockSpec((SUB, D_BR), lambda c: (prev(c), PC_CV // nb)),
                  pl.BlockSpec((SUB, LANE), lambda c: (prev(c), PC_WA)),
                  pl.BlockSpec((3, D_BR), lambda c: (0, 0)),
                  row(LANE), row(D_BR),
                  pl.BlockSpec((LANE, D_BR), lambda c: (0, 0)),
                  row(D_BR),
                  pl.BlockSpec((LANE, D_BR), lambda c: (0, 0)),
                  row(D_BR), row(D_BR), row(D_BR)],
        out_specs=[mat, mat, big, big, big],
        out_shape=[jax.ShapeDtypeStruct((nc, N_PAIR, LANE, LANE), BF16),
                   jax.ShapeDtypeStruct((nc, N_PAIR, LANE, LANE), F32),
                   jax.ShapeDtypeStruct((S, D_BR), BF16),
                   jax.ShapeDtypeStruct((S, D_BR), BF16),
                   jax.ShapeDtypeStruct((S, D_BR), BF16)],
        compiler_params=_cparams(("parallel",)),
        name="rwkv_chunks",
    )(P, P, P, P, P, P, P, P, mu3, muwa, w0, wup, a0, aup, k_k, k_a, r_k)


def _rwkv_seq_kernel(mt_ref, nt_ref, rh_ref, yl_ref, bonus_ref, gc_ref,
                     lnw_ref, lnb_ref, o_ref, z_ref, y_ref, *, NCH):
    i = pl.program_id(0)

    @pl.when(i == 0)
    def _():
        z_ref[...] = jnp.zeros_like(z_ref)

    zs = [z_ref[p] for p in range(N_PAIR)]
    for c in range(NCH):
        rows = slice(c * CHUNK, (c + 1) * CHUNK)
        zb = [z.astype(BF16) for z in zs]
        zs = [jnp.dot(mt_ref[c, p], zb[p], preferred_element_type=F32) + nt_ref[c, p]
              for p in range(N_PAIR)]
        for p in range(N_PAIR):
            sl = slice(p * LANE, (p + 1) * LANE)
            y_ref[rows, sl] = jnp.dot(rh_ref[rows, sl], zb[p], preferred_element_type=F32) + yl_ref[rows, sl]
    for p in range(N_PAIR):
        z_ref[p] = zs[p]

    y = y_ref[...]
    inv_n = 1.0 / N_C
    mu = _head_sum(y) * inv_n
    yc = y - mu
    var = _head_sum(yc * yc) * inv_n
    yn = yc * lax.rsqrt(var + GN_EPS) * lnw_ref[...] + lnb_ref[...]
    o_ref[...] = ((yn + bonus_ref[...]) * _silu(gc_ref[...])).astype(o_ref.dtype)


def _rwkv_seq(mt, nt, rh, yl, bonus, P, ln_w, ln_b):
    S = rh.shape[0]
    T = min(256, S)
    nch = T // CHUNK
    nb = D_BR // LANE
    big = pl.BlockSpec((T, D_BR), lambda i: (i, 0))
    mat = pl.BlockSpec((nch, N_PAIR, LANE, LANE), lambda i: (i, 0, 0, 0))
    row = pl.BlockSpec((1, D_BR), lambda i: (0, 0))
    return pl.pallas_call(
        functools.partial(_rwkv_seq_kernel, NCH=nch),
        grid=(S // T,),
        in_specs=[mat, mat, big, big, big,
                  pl.BlockSpec((T, D_BR), lambda i: (i, PC_GC // nb)),
                  row, row],
        out_specs=big,
        out_shape=jax.ShapeDtypeStruct((S, D_BR), BF16),
        scratch_shapes=[pltpu.VMEM((N_PAIR, LANE, LANE), F32),
                        pltpu.VMEM((T, D_BR), F32)],
        compiler_params=_cparams(("arbitrary",)),
        name="rwkv_seq",
    )(mt, nt, rh, yl, bonus, P, ln_w, ln_b)


def _merge_kernel(h_ref, wg0_ref, wg1_ref, wg2_ref, bm_ref, ya_ref, yb_ref, yc_ref,
                  wb0_ref, wb1_ref, wb2_ref, o_ref):
    h = h_ref[...]
    bm = bm_ref[...]
    acc = None
    for n, (wg_ref, y_ref, wb_ref) in enumerate(((wg0_ref, ya_ref, wb0_ref),
                                                  (wg1_ref, yb_ref, wb1_ref),
                                                  (wg2_ref, yc_ref, wb2_ref))):
        gate = _sigmoid(jnp.dot(h, wg_ref[...], preferred_element_type=F32) + bm[n:n + 1, :])
        term = gate * jnp.dot(y_ref[...], wb_ref[...], preferred_element_type=F32)
        acc = term if acc is None else acc + term
    o_ref[...] = acc.astype(o_ref.dtype)


def _merge(h, w_all, layer, b_merge, ya, yb, yc, w_br):
    S = h.shape[0]
    tm = min(1024, S)
    tn = 512
    nj = D_MODEL // tn
    ysp = pl.BlockSpec((tm, D_BR), lambda i, j: (i, 0))
    wgate = lambda n: pl.BlockSpec((None, D_MODEL, tn), lambda i, j: (layer, 0, n * nj + j))
    return pl.pallas_call(
        _merge_kernel,
        grid=(S // tm, nj),
        in_specs=[pl.BlockSpec((tm, D_MODEL), lambda i, j: (i, 0)),
                  wgate(0), wgate(1), wgate(2),
                  pl.BlockSpec((3, tn), lambda i, j: (0, j)),
                  ysp, ysp, ysp,
                  pl.BlockSpec((None, D_BR, tn), lambda i, j: (0, 0, j)),
                  pl.BlockSpec((None, D_BR, tn), lambda i, j: (1, 0, j)),
                  pl.BlockSpec((None, D_BR, tn), lambda i, j: (2, 0, j))],
        out_specs=pl.BlockSpec((tm, tn), lambda i, j: (i, j)),
        out_shape=jax.ShapeDtypeStruct((S, D_MODEL), BF16),
        compiler_params=pltpu.CompilerParams(dimension_semantics=("parallel", "parallel"),
                                             vmem_limit_bytes=FOX_VMEM),
        name="merge",
    )(h, w_all, w_all, w_all, b_merge, ya, yb, yc, w_br, w_br, w_br)


def _out_kernel(m_ref, w_ref, x_ref, g_ref, o_ref):
    y = jnp.dot(m_ref[...], w_ref[...], preferred_element_type=F32)
    yn = y * lax.rsqrt(jnp.mean(y * y, axis=-1, keepdims=True) + NORM_EPS)
    o_ref[...] = x_ref[...] + yn * g_ref[...]


def _out_proj(m, w_out, x, g):
    S = m.shape[0]
    tm = min(512, S)
    blk = pl.BlockSpec((tm, D_MODEL), lambda i: (i, 0))
    return pl.pallas_call(
        _out_kernel,
        grid=(S // tm,),
        in_specs=[blk,
                  pl.BlockSpec((D_MODEL, D_MODEL), lambda i: (0, 0)),
                  blk,
                  pl.BlockSpec((1, D_MODEL), lambda i: (0, 0))],
        out_specs=blk,
        out_shape=jax.ShapeDtypeStruct((S, D_MODEL), F32),
        compiler_params=_cparams(("parallel",)),
        name="out_proj",
    )(m, w_out, x, g.reshape(1, D_MODEL))


def _pack_plan():
    xa, qkv, fl = 0, 2 * D_BR, 5 * D_BR
    gb = fl + H_B
    pc = gb + D_BR
    gc = pc + 3 * D_BR + 2 * LORA
    mg = gc + D_BR
    groups = [(mg, 3 * D_MODEL), (qkv, 3 * D_BR), (xa, 2 * D_BR), (gb, D_BR), (gc, D_BR),
              (pc, 3 * D_BR + 2 * LORA)]
    src, mode = [], []
    for col0, width in groups:
        for j in range(width // LANE):
            c = col0 + j * LANE
            src.append(c // LANE)
            mode.append(0 if c % LANE == 0 else 1)
            assert c % LANE in (0, H_B)
    src.append(fl // LANE)
    mode.append(2)
    n_pad = P_COLS // LANE - (len(src) - (3 * D_MODEL + 3 * D_BR) // LANE)
    src += [0] * n_pad
    mode += [3] * n_pad
    return src, mode


def _pack_kernel(src_ref, mode_ref, a_ref, b_ref, o_ref):
    mode = mode_ref[pl.program_id(1)]
    a, b = a_ref[...], b_ref[...]
    lane = lax.broadcasted_iota(jnp.int32, a.shape, 1)
    shifted = jnp.where(lane < LANE - H_B, pltpu.roll(a, LANE - H_B, 1), pltpu.roll(b, LANE - H_B, 1))
    out = jnp.where(mode == 0, a, jnp.where(mode == 1, shifted, jnp.where(lane < H_B, a, 0.0)))
    o_ref[...] = jnp.where(mode == 3, 0.0, out).astype(o_ref.dtype)


def _pack_w_in(w_in):
    depth, K, n_in = w_in.shape
    src, mode = _pack_plan()
    last = (n_in - 1) // LANE
    blk = lambda extra: pl.BlockSpec(
        (None, K, LANE), lambda l, ob, src_ref, mode_ref: (l, 0, jnp.minimum(src_ref[ob] + extra, last)))
    return pl.pallas_call(
        _pack_kernel,
        grid_spec=pltpu.PrefetchScalarGridSpec(
            num_scalar_prefetch=2,
            grid=(depth, len(src)),
            in_specs=[blk(0), blk(1)],
            out_specs=pl.BlockSpec((None, K, LANE), lambda l, ob, src_ref, mode_ref: (l, 0, ob))),
        out_shape=jax.ShapeDtypeStruct((depth, K, len(src) * LANE), BF16),
        compiler_params=_cparams(("parallel", "parallel")),
        name="pack_w_in",
    )(jnp.asarray(src, jnp.int32), jnp.asarray(mode, jnp.int32), w_in, w_in)


W_COL_QKV = 3 * D_MODEL
W_COL_P = 3 * D_MODEL + 3 * D_BR


def _layer_weights(lru_gate_w, rwkv_w_up, rwkv_a_up):
    eye = jnp.eye(H_A, dtype=F32)
    wg = jnp.einsum('ghij,hk->ghikj', lru_gate_w, eye).reshape(2, D_BR, D_BR)
    wg = jnp.concatenate([wg[0], wg[1]], axis=1).astype(BF16)
    z = jnp.zeros((LORA, D_BR), F32)
    wup = jnp.concatenate([rwkv_w_up, z], axis=0).astype(BF16)
    aup = jnp.concatenate([z, rwkv_a_up], axis=0).astype(BF16)
    return wg, wup, aup


def kernel(x, pre_norm_w, post_norm_w, w_in, b_merge, conv_w, conv_b, lru_gate_w, lru_gate_b, lru_lambda,
           fox_b_f, rwkv_mu, rwkv_w0, rwkv_w_up, rwkv_a0, rwkv_a_up, rwkv_k_k, rwkv_k_a, rwkv_r_k,
           rwkv_ln_w, rwkv_ln_b, w_branch, w_out):
    B, S, _ = x.shape
    assert B == 1 and S % CHUNK == 0
    depth = w_in.shape[0]
    w_all = _pack_w_in(w_in)
    xs = x.reshape(S, D_MODEL)
    ones_p = jnp.ones((1, P_COLS), F32)
    qscale = jnp.concatenate([jnp.full((1, D_BR), DH_B ** -0.5 * LOG2E, F32), jnp.ones((1, 2 * D_BR), F32)], axis=1)
    row = lambda a: a.reshape(1, -1)
    for l in range(depth):
        wg, wup, aup = _layer_weights(lru_gate_w[l], rwkv_w_up[l], rwkv_a_up[l])
        h = _rmsnorm_bf16(xs, pre_norm_w[l])
        P = _proj(h, w_all, l, W_COL_P, ones_p, F32, 1536)
        qkv = _proj(h, w_all, l, W_COL_QKV, qscale, BF16, 1024)
        ya = _rglru(P, conv_w[l], row(conv_b[l]), wg, row(lru_gate_b[l]), row(lru_lambda[l]))
        bf_pad = jnp.concatenate([fox_b_f[l], jnp.zeros((LANE - H_B,), F32)]).reshape(1, LANE)
        kx, cs = _fox_c(P, bf_pad)
        yb = _fox(qkv, kx, cs, P)
        mu = rwkv_mu[l]
        mu3 = mu[:3 * D_BR].reshape(3, D_BR)
        muwa = mu[3 * D_BR:].reshape(1, LANE)
        mt, nt, rh, yl, bonus = _rwkv_chunks(P, mu3, muwa, row(rwkv_w0[l]), wup, row(rwkv_a0[l]), aup,
                                             row(rwkv_k_k[l]), row(rwkv_k_a[l]), row(rwkv_r_k[l]))
        yc = _rwkv_seq(mt, nt, rh, yl, bonus, P, row(rwkv_ln_w[l]), row(rwkv_ln_b[l]))
        m = _merge(h, w_all, l, b_merge[l], ya, yb, yc, w_branch[l].astype(BF16))
        xs = _out_proj(m, w_out[l].astype(BF16), xs, post_norm_w[l])
    return xs.reshape(B, S, D_MODEL)
```

```python
import functools

import jax
import jax.numpy as jnp
from jax import lax
from jax.experimental import pallas as pl
from jax.experimental.pallas import tpu as pltpu

F32 = jnp.float32
BF16 = jnp.bfloat16

D_MODEL = 2048
D_BR = 1024
H_A, BW_A = 16, 64
CONV_W = 4
LRU_C = 8.0
H_B, DH_B = 8, 128
H_C, N_C = 16, 64
LORA = 64
NORM_EPS = 1e-6
GN_EPS = 64e-5
LANE = 128
SUB = 8
CHUNK = 64
N_PAIR = H_C // 2
RW_NCH = 2
NEG = -1e30
LOG2E = 1.4426950408889634
FOX_Q = 1024
FOX_T = 1024
FOX_NB = 2
FOX_VMEM = 56 * 1024 * 1024

PC_XA, PC_GA, PC_GB, PC_GC, PC_CR, PC_CK, PC_CV = 0, 8, 16, 24, 32, 40, 48
PC_WA, PC_FL = 56, 57
P_COLS = 60 * LANE

VMEM_LIMIT = 48 * 1024 * 1024


def _cparams(sem):
    return pltpu.CompilerParams(dimension_semantics=sem, vmem_limit_bytes=VMEM_LIMIT)


def _sigmoid(x):
    return 0.5 * jnp.tanh(0.5 * x) + 0.5


def _silu(x):
    return x * _sigmoid(x)


def _softplus(z):
    return jnp.maximum(z, 0.0) + jnp.log1p(jnp.exp(-jnp.abs(z)))


def _split3(x):
    hi = x.astype(BF16)
    r1 = x - hi.astype(F32)
    mid = r1.astype(BF16)
    return hi, mid, (r1 - mid.astype(F32)).astype(BF16)


def _tri_cumsum(lower_mask, x):
    tri = lower_mask.astype(BF16)
    hi, mid, lo = _split3(x)
    return (jnp.dot(tri, hi, preferred_element_type=F32) + jnp.dot(tri, mid, preferred_element_type=F32)
            + jnp.dot(tri, lo, preferred_element_type=F32))


def _norm_kernel(x_ref, g_ref, o_ref):
    x = x_ref[...]
    y = x * lax.rsqrt(jnp.mean(x * x, axis=-1, keepdims=True) + NORM_EPS)
    o_ref[...] = (y * g_ref[...]).astype(o_ref.dtype)


def _rmsnorm_bf16(x, g):
    S = x.shape[0]
    tm = min(512, S)
    return pl.pallas_call(
        _norm_kernel,
        grid=(S // tm,),
        in_specs=[pl.BlockSpec((tm, D_MODEL), lambda i: (i, 0)),
                  pl.BlockSpec((1, D_MODEL), lambda i: (0, 0))],
        out_specs=pl.BlockSpec((tm, D_MODEL), lambda i: (i, 0)),
        out_shape=jax.ShapeDtypeStruct((S, D_MODEL), BF16),
        compiler_params=_cparams(("parallel",)),
        name="rmsnorm",
    )(x, g.reshape(1, D_MODEL))


def _proj_kernel(h_ref, w_ref, s_ref, o_ref):
    acc = jnp.dot(h_ref[...], w_ref[...], preferred_element_type=F32)
    o_ref[...] = (acc * s_ref[...]).astype(o_ref.dtype)


def _proj(h, w_all, layer, col0, colscale, out_dtype, tn):
    S, K = h.shape
    N = colscale.shape[1]
    tm = min(1024, S)
    j0 = col0 // tn
    assert col0 % tn == 0 and N % tn == 0
    return pl.pallas_call(
        _proj_kernel,
        grid=(S // tm, N // tn),
        in_specs=[pl.BlockSpec((tm, K), lambda i, j: (i, 0)),
                  pl.BlockSpec((None, K, tn), lambda i, j: (layer, 0, j0 + j)),
                  pl.BlockSpec((1, tn), lambda i, j: (0, j))],
        out_specs=pl.BlockSpec((tm, tn), lambda i, j: (i, j)),
        out_shape=jax.ShapeDtypeStruct((S, N), out_dtype),
        compiler_params=_cparams(("parallel", "parallel")),
        name="in_proj",
    )(h, w_all, colscale)


def _rglru_kernel(xa_ref, ga_ref, cw_ref, cb_ref, wg_ref, gbias_ref, lam_ref, o_ref,
                  xe_ref, hc_ref, a_ref, b_ref, *, T):
    i = pl.program_id(0)

    @pl.when(i == 0)
    def _():
        xe_ref[...] = jnp.zeros_like(xe_ref)
        hc_ref[...] = jnp.zeros_like(hc_ref)

    G = T // SUB
    sub = lax.broadcasted_iota(jnp.int32, (G, SUB, D_BR), 1)

    def rot(v, d):
        return pltpu.roll(v, d, 1)

    x = xa_ref[...].reshape(G, SUB, D_BR)
    x_prev = jnp.concatenate([xe_ref[...].reshape(1, SUB, D_BR), x[:G - 1]], axis=0)
    xe_ref[...] = xa_ref[T - SUB:T, :]
    cw = cw_ref[...]
    xc = cb_ref[...] + x * cw[3:4]
    for d in range(1, CONV_W):
        xc = xc + jnp.where(sub < d, rot(x_prev, d), rot(x, d)) * cw[3 - d:4 - d]
    xc = xc.reshape(T, D_BR)

    gates = jnp.dot(xc.astype(BF16), wg_ref[...], preferred_element_type=F32) + gbias_ref[...]
    r = _sigmoid(gates[:, :D_BR])
    ig = _sigmoid(gates[:, D_BR:])
    log_a = (-LRU_C) * r * _softplus(-lam_ref[...])
    a = jnp.exp(log_a)
    y = -jnp.tanh(log_a) * (1.0 + a * a)
    mult = jnp.where(y > 0.0, y * lax.rsqrt(y), 0.0)
    rows = lax.broadcasted_iota(jnp.int32, (T, D_BR), 0)
    mult = jnp.where((rows + i * T) == 0, 1.0, mult)
    b = (mult * ig * xc).reshape(G, SUB, D_BR)
    a = a.reshape(G, SUB, D_BR)

    for d in (1, 2, 4):
        keep = sub >= d
        b = jnp.where(keep, a * rot(b, d) + b, b)
        a = jnp.where(keep, a * rot(a, d), a)
    a_ref[...] = a.reshape(T, D_BR)
    b_ref[...] = b.reshape(T, D_BR)
    carry = hc_ref[...]
    for g in range(T // SUB):
        sl = slice(g * SUB, (g + 1) * SUB)
        hg = b_ref[sl, :] + a_ref[sl, :] * carry
        b_ref[sl, :] = hg
        carry = jnp.broadcast_to(hg[SUB - 1:SUB, :], (SUB, D_BR))
    hc_ref[...] = carry
    o_ref[...] = (b_ref[...] * _silu(ga_ref[...])).astype(o_ref.dtype)


def _rglru(P, conv_w, conv_b, wg, gbias, lam):
    S = P.shape[0]
    T = min(256, S)
    nb = D_BR // LANE
    return pl.pallas_call(
        functools.partial(_rglru_kernel, T=T),
        grid=(S // T,),
        in_specs=[pl.BlockSpec((T, D_BR), lambda i: (i, PC_XA // nb)),
                  pl.BlockSpec((T, D_BR), lambda i: (i, PC_GA // nb)),
                  pl.BlockSpec((CONV_W, D_BR), lambda i: (0, 0)),
                  pl.BlockSpec((1, D_BR), lambda i: (0, 0)),
                  pl.BlockSpec((D_BR, 2 * D_BR), lambda i: (0, 0)),
                  pl.BlockSpec((1, 2 * D_BR), lambda i: (0, 0)),
                  pl.BlockSpec((1, D_BR), lambda i: (0, 0))],
        out_specs=pl.BlockSpec((T, D_BR), lambda i: (i, 0)),
        out_shape=jax.ShapeDtypeStruct((S, D_BR), BF16),
        scratch_shapes=[pltpu.VMEM((SUB, D_BR), F32),
                        pltpu.VMEM((SUB, D_BR), F32),
                        pltpu.VMEM((T, D_BR), F32),
                        pltpu.VMEM((T, D_BR), F32)],
        compiler_params=_cparams(("arbitrary",)),
        name="rglru",
    )(P, P, conv_w, conv_b, wg, gbias, lam)


def _fox_c_kernel(fl_ref, bf_ref, kx_ref, cs_ref, carry_ref, *, T):
    i = pl.program_id(0)

    @pl.when(i == 0)
    def _():
        carry_ref[...] = jnp.zeros_like(carry_ref)

    z = fl_ref[...] + bf_ref[...]
    lf = (jnp.minimum(z, 0.0) - jnp.log1p(jnp.exp(-jnp.abs(z)))) * LOG2E
    TB = min(256, T)
    rr = lax.broadcasted_iota(jnp.int32, (TB, TB), 0)
    cc = lax.broadcasted_iota(jnp.int32, (TB, TB), 1)
    blocks, off = [], jnp.zeros((1, LANE), F32)
    for b in range(T // TB):
        blk = _tri_cumsum(cc <= rr, lf[b * TB:(b + 1) * TB]) + off
        blocks.append(blk)
        off = blk[TB - 1:TB, :]
    lc = jnp.concatenate(blocks, axis=0)
    carry = carry_ref[...]
    cs_ref[0] = carry
    carry_ref[...] = carry + jnp.broadcast_to(off, (SUB, LANE))
    lane = lax.broadcasted_iota(jnp.int32, (T, LANE), 1)
    hi, mid, lo = [jnp.where(lane < H_B, pc.astype(F32), 0.0) for pc in _split3(-lc)]
    packed = (hi + pltpu.roll(mid, H_B, 1) + pltpu.roll(lo, 2 * H_B, 1)).astype(BF16)
    rs = lax.broadcasted_iota(jnp.int32, (LANE, D_BR), 0)
    cs = lax.broadcasted_iota(jnp.int32, (LANE, D_BR), 1)
    sel = ((cs == (rs % H_B) * LANE + rs // H_B) & (rs < 3 * H_B)).astype(BF16)
    kx_ref[...] = jnp.dot(packed, sel, preferred_element_type=F32).astype(BF16)


def _fox_c(P, bf_pad):
    S = P.shape[0]
    T = min(FOX_T, FOX_Q, S // 2)
    return pl.pallas_call(
        functools.partial(_fox_c_kernel, T=T),
        grid=(S // T,),
        in_specs=[pl.BlockSpec((T, LANE), lambda i: (i, PC_FL)),
                  pl.BlockSpec((1, LANE), lambda i: (0, 0))],
        out_specs=[pl.BlockSpec((T, D_BR), lambda i: (i, 0)),
                   pl.BlockSpec((1, SUB, LANE), lambda i: (i, 0, 0))],
        out_shape=[jax.ShapeDtypeStruct((S, D_BR), BF16),
                   jax.ShapeDtypeStruct((S // T, SUB, LANE), F32)],
        scratch_shapes=[pltpu.VMEM((SUB, LANE), F32)],
        compiler_params=_cparams(("arbitrary",)),
        name="fox_cumgate",
    )(P, bf_pad)


def _fox_kernel(cs_ref, q_ref, k_ref, kx_ref, v_ref, gb_ref, o_ref, sh_ref, dp_ref, acc_ref, *, Q, T, NB):
    h = pl.program_id(0)
    i = pl.program_id(1)
    assert Q == T and NB == 2
    lane = lax.broadcasted_iota(jnp.int32, (Q, DH_B), 1)
    qa = jnp.concatenate([q_ref[...], jnp.where(lane < 3, 1.0, 0.0).astype(BF16)], axis=1)
    ones_v = jnp.ones((T, DH_B), BF16)
    c_i = cs_ref[h, i]

    def chunk(j):
        ks = pl.multiple_of(j * T, T)
        kc = jnp.concatenate([k_ref[pl.ds(ks, T), :], kx_ref[pl.ds(ks, T), :]], axis=1)
        vc = jnp.concatenate([v_ref[pl.ds(ks, T), :], ones_v], axis=1)
        return kc, vc, c_i - cs_ref[h, j]

    def qk(kc):
        return lax.dot_general(qa, kc, (((1,), (1,)), ((), ())), preferred_element_type=F32)

    def sm_pv(carry, s, vc, d):
        sh, d_prev, acc = carry
        a = sh + (d_prev - d)
        sh_new = jnp.maximum(a, jnp.max(s, axis=-1, keepdims=True))
        p = jnp.exp2(s - sh_new)
        acc = jnp.exp2(a - sh_new) * acc + jnp.dot(p.astype(BF16), vc, preferred_element_type=F32)
        return sh_new, d, acc

    def chunks(j0, n, carry, diag=False):
        cks = [chunk(j0 + c) for c in range(n)]
        s_next = qk(cks[0][0])
        for c in range(n):
            s = s_next
            if c + 1 < n:
                s_next = qk(cks[c + 1][0])
            elif diag:
                rows = lax.broadcasted_iota(jnp.int32, (Q, T), 0)
                cols = lax.broadcasted_iota(jnp.int32, (Q, T), 1)
                s = jnp.where(cols <= rows, s, NEG)
            carry = sm_pv(carry, s, cks[c][1], cks[c][2])
        return carry

    carry = (jnp.full((Q, 1), NEG, F32), jnp.float32(0.0), jnp.zeros((Q, 2 * DH_B), F32))
    n = i + 1
    tail = jnp.where(n % 2 == 0, 2, jnp.where(n == 1, 1, 3))
    sh, d_prev, acc = lax.fori_loop(0, (n - tail) // 2, lambda jj, c: chunks(2 * jj, 2, c), carry)
    sh_ref[...] = sh
    dp_ref[0] = d_prev
    acc_ref[...] = acc
    for t in (1, 2, 3):
        @pl.when(tail == t)
        def _():
            acc_ref[...] = chunks(n - t, t, (sh_ref[...], dp_ref[0], acc_ref[...]), diag=True)[2]
    acc = acc_ref[...]
    o_ref[...] = (acc[:, :DH_B] / acc[:, DH_B:] * _silu(gb_ref[...])).astype(o_ref.dtype)


def _fox(qkv, kx, cs, P):
    S = qkv.shape[0]
    Q = min(FOX_Q, S // 2)
    T = min(FOX_T, Q)
    cst = cs[:, 0, :H_B].T
    kv = lambda col0: pl.BlockSpec((S, DH_B), lambda h, i: (0, col0 + h))
    return pl.pallas_call(
        functools.partial(_fox_kernel, Q=Q, T=T, NB=FOX_NB),
        grid=(H_B, S // Q),
        in_specs=[pl.BlockSpec(memory_space=pltpu.SMEM),
                  pl.BlockSpec((Q, DH_B), lambda h, i: (i, h)),
                  kv(H_B), kv(0), kv(2 * H_B),
                  pl.BlockSpec((Q, DH_B), lambda h, i: (i, PC_GB + h))],
        out_specs=pl.BlockSpec((Q, DH_B), lambda h, i: (i, h)),
        out_shape=jax.ShapeDtypeStruct((S, D_BR), BF16),
        scratch_shapes=[pltpu.VMEM((Q, 1), F32), pltpu.SMEM((1,), F32),
                        pltpu.VMEM((Q, 2 * DH_B), F32)],
        compiler_params=pltpu.CompilerParams(dimension_semantics=("parallel", "parallel"),
                                             vmem_limit_bytes=FOX_VMEM),
        name="fox_attn",
    )(cst, qkv, qkv, kx, qkv, P)


def _head_sum(x):
    rr = lax.broadcasted_iota(jnp.int32, (LANE, LANE), 0) // N_C
    cc = lax.broadcasted_iota(jnp.int32, (LANE, LANE), 1) // N_C
    ones_bd = (rr == cc).astype(BF16)
    hi = x.astype(BF16)
    lo = (x - hi.astype(F32)).astype(BF16)
    outs = []
    for p in range(N_PAIR):
        sl = slice(p * LANE, (p + 1) * LANE)
        outs.append(jnp.dot(hi[:, sl], ones_bd, preferred_element_type=F32)
                    + jnp.dot(lo[:, sl], ones_bd, preferred_element_type=F32))
    return jnp.concatenate(outs, axis=1)


def _rwkv_tokens(cr_ref, ck_ref, cv_ref, wa_ref, pr_ref, pk_ref, pv_ref, pwa_ref,
                 mu_ref, muwa_ref, w0_ref, wup_ref, a0_ref, aup_ref, kk_ref, ka_ref):
    live = (pl.program_id(0) > 0).astype(F32)

    def shift_mix(cur_ref, prev_ref, mu):
        cur = cur_ref[...]
        rows = lax.broadcasted_iota(jnp.int32, cur.shape, 0)
        first = jnp.broadcast_to(prev_ref[SUB - 1:SUB, :] * live, cur.shape)
        prev = jnp.where(rows == 0, first, pltpu.roll(cur, 1, 0))
        return cur + (prev - cur) * mu

    r = shift_mix(cr_ref, pr_ref, mu_ref[0:1, :])
    kraw = shift_mix(ck_ref, pk_ref, mu_ref[1:2, :])
    v = shift_mix(cv_ref, pv_ref, mu_ref[2:3, :])
    wa = shift_mix(wa_ref, pwa_ref, muwa_ref[...])
    zw = w0_ref[...] + jnp.dot(jnp.tanh(wa).astype(BF16), wup_ref[...], preferred_element_type=F32)
    za = a0_ref[...] + jnp.dot(wa.astype(BF16), aup_ref[...], preferred_element_type=F32)
    g = (-0.6065306597126334) * _sigmoid(zw)
    a = _sigmoid(za)
    kk = kraw * kk_ref[...]
    ss = _head_sum(kk * kk)
    kkn = kk * lax.rsqrt(jnp.maximum(ss, 1e-24))
    return r, kraw * (1.0 + (a - 1.0) * ka_ref[...]), v, kkn, kkn * a, g


def _mm(a, b):
    return jnp.dot(a.astype(BF16), b.astype(BF16), preferred_element_type=F32)


def _mm_nt(a, b):
    return lax.dot_general(a.astype(BF16), b.astype(BF16), (((1,), (1,)), ((), ())),
                           preferred_element_type=F32)


def _mm_tn(a, b):
    return lax.dot_general(a.astype(BF16), b.astype(BF16), (((0,), (0,)), ((), ())),
                           preferred_element_type=F32)


def _rwkv_chunk_kernel(*refs, NCH):
    rk_ref, mt_ref, nt_ref, rh_ref, yl_ref, bonus_ref = refs[16:]
    L = CHUNK
    rv_a, kv_a, vm_a, kkn_a, bv_a, g_a = _rwkv_tokens(*refs[:16])
    bonus_ref[...] = (_head_sum(rv_a * kv_a * rk_ref[...]) * vm_a).astype(bonus_ref.dtype)
    rr = lax.broadcasted_iota(jnp.int32, (L, L), 0)
    cc = lax.broadcasted_iota(jnp.int32, (L, L), 1)
    ops = []
    for ch in range(NCH):
        rows = slice(ch * L, (ch + 1) * L)
        rv, kv, vm, bv, g = rv_a[rows], kv_a[rows], vm_a[rows], bv_a[rows], g_a[rows]
        G = _tri_cumsum(cc <= rr, g)
        Ge = G - g
        Gm = G[L // 2 - 1:L // 2, :]
        GL = G[L - 1:L, :]
        na = -kkn_a[rows]
        e_b = jnp.exp(Gm - G)
        e_l = jnp.exp(GL - G)
        ops.append(dict(al=na * jnp.exp(Ge), al_m=na * jnp.exp(Ge - Gm), be_m=bv * e_b, ka_m=kv * e_b,
                        rho=rv * jnp.exp(G), rho_m=rv * jnp.exp(G - Gm), b_hat=bv * e_l, k_hat=kv * e_l,
                        d_row=jnp.exp(GL), vm=vm))

    lo1 = lax.broadcasted_iota(jnp.int32, (L, LANE), 1) < N_C
    r4 = lax.broadcasted_iota(jnp.int32, (4 * L, LANE), 0)
    c4 = lax.broadcasted_iota(jnp.int32, (4 * L, LANE), 1) & (N_C - 1)
    tri4 = c4 < (r4 & (L - 1)) + jnp.where(r4 < 2 * L, 0, 1)
    rb = lax.broadcasted_iota(jnp.int32, (LANE, LANE), 0)
    cb = lax.broadcasted_iota(jnp.int32, (LANE, LANE), 1)
    bd = (rb // N_C) == (cb // N_C)
    eye = rb == cb
    zeros_l = jnp.zeros((L, LANE), BF16)

    units = [(ch, p) for ch in range(NCH) for p in range(N_PAIR)]
    nu = len(units)
    heads = [(n, hh) for n in range(nu) for hh in range(2)]
    sls = [slice(p * LANE, (p + 1) * LANE) for p in range(N_PAIR)]
    get = lambda name, u: ops[u[0]][name][:, sls[u[1]]]
    vms = [get('vm', u).astype(BF16) for u in units]
    a_cat = []
    for u in units:
        alm_p, rhm_p = get('al_m', u), get('rho_m', u)
        lhs4 = jnp.concatenate([jnp.where(lo1, alm_p, 0.0), jnp.where(lo1, 0.0, alm_p),
                                jnp.where(lo1, rhm_p, 0.0), jnp.where(lo1, 0.0, rhm_p)], axis=0)
        rhs2 = jnp.concatenate([get('be_m', u), get('ka_m', u)], axis=0)
        a_cat.append(jnp.where(tri4, _mm_nt(lhs4, rhs2), 0.0).astype(BF16))
    zv = [jnp.concatenate([zeros_l, v], axis=0) for v in vms]
    akv = [_mm(jnp.where(lo1, 0.0, a_cat[n][hh * L:(hh + 1) * L]), zv[n]) for n, hh in heads]
    xs, apow = [], []
    for n, u in enumerate(units):
        al_p = get('al', u)
        akv_p = jnp.where(lo1, akv[2 * n], akv[2 * n + 1])
        xs.append(jnp.where(lo1, al_p, pltpu.roll(akv_p, N_C, 1)))
        xs.append(jnp.where(lo1, pltpu.roll(al_p, N_C, 1), akv_p))
        for hh in range(2):
            apow.append(jnp.where(lo1, a_cat[n][hh * L:(hh + 1) * L], 0.0).astype(BF16))
    for step in range(6):
        if step < 5:
            res = [_mm(a[:, :N_C], jnp.concatenate([x.astype(BF16), a], axis=1)) for a, x in zip(apow, xs)]
            apow = [r[:, LANE:].astype(BF16) for r in res]
        else:
            res = [_mm(a[:, :N_C], x) for a, x in zip(apow, xs)]
        xs = [x + r[:, :LANE] for x, r in zip(xs, res)]
    w_pq = []
    for n in range(nu):
        x0, x1 = xs[2 * n], xs[2 * n + 1]
        p_pair = jnp.where(lo1, x0, pltpu.roll(x1, N_C, 1)).astype(BF16)
        q_pair = jnp.where(lo1, pltpu.roll(x0, N_C, 1), x1).astype(BF16)
        w_pq.append(jnp.concatenate([jnp.concatenate([p_pair, q_pair], axis=1),
                                     jnp.concatenate([zeros_l, vms[n]], axis=1)], axis=0))
    rbs = [_mm(a_cat[n][(2 + hh) * L:(3 + hh) * L], w_pq[n]) for n, hh in heads]
    gs = [_mm_tn(jnp.concatenate([get('b_hat', u), get('k_hat', u)], axis=0), w_pq[n])
          for n, u in enumerate(units)]
    for n, (ch, p) in enumerate(units):
        sl = sls[p]
        rows = slice(ch * L, (ch + 1) * L)
        r0, r1 = rbs[2 * n], rbs[2 * n + 1]
        rh_ref[rows, sl] = (get('rho', (ch, p)) + jnp.where(lo1, r0[:, :LANE], r1[:, :LANE])).astype(rh_ref.dtype)
        yl_ref[rows, sl] = jnp.where(lo1, r0[:, LANE:], r1[:, LANE:]).astype(yl_ref.dtype)
        d_p = jnp.broadcast_to(ops[ch]['d_row'][:, sl], (LANE, LANE))
        mt_ref[ch, p] = (jnp.where(eye, d_p, 0.0) + jnp.where(bd, gs[n][:, :LANE], 0.0)).astype(mt_ref.dtype)
        nt_ref[ch, p] = jnp.where(bd, gs[n][:, LANE:], 0.0)


def _rwkv_chunks(P, mu3, muwa, w0, wup, a0, aup, k_k, k_a, r_k):
    S = P.shape[0]
    nc = S // CHUNK
    nb = D_BR // LANE
    T = RW_NCH * CHUNK
    prev = lambda c: jnp.maximum(c * (T // SUB) - 1, 0)
    row = lambda n: pl.BlockSpec((1, n), lambda c: (0, 0))
    big = pl.BlockSpec((T, D_BR), lambda c: (c, 0))
    mat = pl.BlockSpec((RW_NCH, N_PAIR, LANE, LANE), lambda c: (c, 0, 0, 0))
    return pl.pallas_call(
        functools.partial(_rwkv_chunk_kernel, NCH=RW_NCH),
        grid=(nc // RW_NCH,),
        in_specs=[pl.BlockSpec((T, D_BR), lambda c: (c, PC_CR // nb)),
                  pl.BlockSpec((T, D_BR), lambda c: (c, PC_CK // nb)),
                  pl.BlockSpec((T, D_BR), lambda c: (c, PC_CV // nb)),
                  pl.BlockSpec((T, LANE), lambda c: (c, PC_WA)),
                  pl.BlockSpec((SUB, D_BR), lambda c: (prev(c), PC_CR // nb)),
                  pl.BlockSpec((SUB, D_BR), lambda c: (prev(c), PC_CK // nb)),
                  pl.BlockSpec((SUB, D_BR), lambda c: (prev(c), PC_CV // nb)),
                  pl.BlockSpec((SUB, LANE), lambda c: (prev(c), PC_WA)),
                  pl.BlockSpec((3, D_BR), lambda c: (0, 0)),
                  row(LANE), row(D_BR),
                  pl.BlockSpec((LANE, D_BR), lambda c: (0, 0)),
                  row(D_BR),
                  pl.BlockSpec((LANE, D_BR), lambda c: (0, 0)),
                  row(D_BR), row(D_BR), row(D_BR)],
        out_specs=[mat, mat, big, big, big],
        out_shape=[jax.ShapeDtypeStruct((nc, N_PAIR, LANE, LANE), BF16),
                   jax.ShapeDtypeStruct((nc, N_PAIR, LANE, LANE), F32),
                   jax.ShapeDtypeStruct((S, D_BR), BF16),
                   jax.ShapeDtypeStruct((S, D_BR), BF16),
                   jax.ShapeDtypeStruct((S, D_BR), BF16)],
        compiler_params=_cparams(("parallel",)),
        name="rwkv_chunks",
    )(P, P, P, P, P, P, P, P, mu3, muwa, w0, wup, a0, aup, k_k, k_a, r_k)


def _rwkv_seq_kernel(mt_ref, nt_ref, rh_ref, yl_ref, bonus_ref, gc_ref,
                     lnw_ref, lnb_ref, o_ref, z_ref, y_ref, *, NCH):
    i = pl.program_id(0)

    @pl.when(i == 0)
    def _():
        z_ref[...] = jnp.zeros_like(z_ref)

    zs = [z_ref[p] for p in range(N_PAIR)]
    for c in range(NCH):
        rows = slice(c * CHUNK, (c + 1) * CHUNK)
        zb = [z.astype(BF16) for z in zs]
        zs = [jnp.dot(mt_ref[c, p], zb[p], preferred_element_type=F32) + nt_ref[c, p]
              for p in range(N_PAIR)]
        for p in range(N_PAIR):
            sl = slice(p * LANE, (p + 1) * LANE)
            y_ref[rows, sl] = jnp.dot(rh_ref[rows, sl], zb[p], preferred_element_type=F32) + yl_ref[rows, sl]
    for p in range(N_PAIR):
        z_ref[p] = zs[p]

    y = y_ref[...]
    inv_n = 1.0 / N_C
    mu = _head_sum(y) * inv_n
    yc = y - mu
    var = _head_sum(yc * yc) * inv_n
    yn = yc * lax.rsqrt(var + GN_EPS) * lnw_ref[...] + lnb_ref[...]
    o_ref[...] = ((yn + bonus_ref[...]) * _silu(gc_ref[...])).astype(o_ref.dtype)


def _rwkv_seq(mt, nt, rh, yl, bonus, P, ln_w, ln_b):
    S = rh.shape[0]
    T = min(256, S)
    nch = T // CHUNK
    nb = D_BR // LANE
    big = pl.BlockSpec((T, D_BR), lambda i: (i, 0))
    mat = pl.BlockSpec((nch, N_PAIR, LANE, LANE), lambda i: (i, 0, 0, 0))
    row = pl.BlockSpec((1, D_BR), lambda i: (0, 0))
    return pl.pallas_call(
        functools.partial(_rwkv_seq_kernel, NCH=nch),
        grid=(S // T,),
        in_specs=[mat, mat, big, big, big,
                  pl.BlockSpec((T, D_BR), lambda i: (i, PC_GC // nb)),
                  row, row],
        out_specs=big,
        out_shape=jax.ShapeDtypeStruct((S, D_BR), BF16),
        scratch_shapes=[pltpu.VMEM((N_PAIR, LANE, LANE), F32),
                        pltpu.VMEM((T, D_BR), F32)],
        compiler_params=_cparams(("arbitrary",)),
        name="rwkv_seq",
    )(mt, nt, rh, yl, bonus, P, ln_w, ln_b)


def _merge_kernel(h_ref, wg0_ref, wg1_ref, wg2_ref, bm_ref, ya_ref, yb_ref, yc_ref,
                  wb0_ref, wb1_ref, wb2_ref, o_ref):
    h = h_ref[...]
    bm = bm_ref[...]
    acc = None
    for n, (wg_ref, y_ref, wb_ref) in enumerate(((wg0_ref, ya_ref, wb0_ref),
                                                  (wg1_ref, yb_ref, wb1_ref),
                                                  (wg2_ref, yc_ref, wb2_ref))):
        gate = _sigmoid(jnp.dot(h, wg_ref[...], preferred_element_type=F32) + bm[n:n + 1, :])
        term = gate * jnp.dot(y_ref[...], wb_ref[...], preferred_element_type=F32)
        acc = term if acc is None else acc + term
    o_ref[...] = acc.astype(o_ref.dtype)


def _merge(h, w_all, layer, b_merge, ya, yb, yc, w_br):
    S = h.shape[0]
    tm = min(1024, S)
    tn = 512
    nj = D_MODEL // tn
    ysp = pl.BlockSpec((tm, D_BR), lambda i, j: (i, 0))
    wgate = lambda n: pl.BlockSpec((None, D_MODEL, tn), lambda i, j: (layer, 0, n * nj + j))
    return pl.pallas_call(
        _merge_kernel,
        grid=(S // tm, nj),
        in_specs=[pl.BlockSpec((tm, D_MODEL), lambda i, j: (i, 0)),
                  wgate(0), wgate(1), wgate(2),
                  pl.BlockSpec((3, tn), lambda i, j: (0, j)),
                  ysp, ysp, ysp,
                  pl.BlockSpec((None, D_BR, tn), lambda i, j: (0, 0, j)),
                  pl.BlockSpec((None, D_BR, tn), lambda i, j: (1, 0, j)),
                  pl.BlockSpec((None, D_BR, tn), lambda i, j: (2, 0, j))],
        out_specs=pl.BlockSpec((tm, tn), lambda i, j: (i, j)),
        out_shape=jax.ShapeDtypeStruct((S, D_MODEL), BF16),
        compiler_params=pltpu.CompilerParams(dimension_semantics=("parallel", "parallel"),
                                             vmem_limit_bytes=FOX_VMEM),
        name="merge",
    )(h, w_all, w_all, w_all, b_merge, ya, yb, yc, w_br, w_br, w_br)


def _out_kernel(m_ref, w_ref, x_ref, g_ref, o_ref):
    y = jnp.dot(m_ref[...], w_ref[...], preferred_element_type=F32)
    yn = y * lax.rsqrt(jnp.mean(y * y, axis=-1, keepdims=True) + NORM_EPS)
    o_ref[...] = x_ref[...] + yn * g_ref[...]


def _out_proj(m, w_out, x, g):
    S = m.shape[0]
    tm = min(512, S)
    blk = pl.BlockSpec((tm, D_MODEL), lambda i: (i, 0))
    return pl.pallas_call(
        _out_kernel,
        grid=(S // tm,),
        in_specs=[blk,
                  pl.BlockSpec((D_MODEL, D_MODEL), lambda i: (0, 0)),
                  blk,
                  pl.BlockSpec((1, D_MODEL), lambda i: (0, 0))],
        out_specs=blk,
        out_shape=jax.ShapeDtypeStruct((S, D_MODEL), F32),
        compiler_params=_cparams(("parallel",)),
        name="out_proj",
    )(m, w_out, x, g.reshape(1, D_MODEL))


def _pack_plan():
    xa, qkv, fl = 0, 2 * D_BR, 5 * D_BR
    gb = fl + H_B
    pc = gb + D_BR
    gc = pc + 3 * D_BR + 2 * LORA
    mg = gc + D_BR
    groups = [(mg, 3 * D_MODEL), (qkv, 3 * D_BR), (xa, 2 * D_BR), (gb, D_BR), (gc, D_BR),
              (pc, 3 * D_BR + 2 * LORA)]
    src, mode = [], []
    for col0, width in groups:
        for j in range(width // LANE):
            c = col0 + j * LANE
            src.append(c // LANE)
            mode.append(0 if c % LANE == 0 else 1)
            assert c % LANE in (0, H_B)
    src.append(fl // LANE)
    mode.append(2)
    n_pad = P_COLS // LANE - (len(src) - (3 * D_MODEL + 3 * D_BR) // LANE)
    src += [0] * n_pad
    mode += [3] * n_pad
    return src, mode


def _pack_kernel(src_ref, mode_ref, a_ref, b_ref, o_ref):
    mode = mode_ref[pl.program_id(1)]

    def emit(t):
        o_ref[...] = t.T.astype(o_ref.dtype)

    @pl.when(mode == 0)
    def _():
        emit(a_ref[...])

    @pl.when(mode == 1)
    def _():
        emit(jnp.concatenate([a_ref[SUB:, :], b_ref[...]], axis=0))

    @pl.when(mode == 2)
    def _():
        a = a_ref[...]
        emit(jnp.where(lax.broadcasted_iota(jnp.int32, a.shape, 0) < H_B, a, 0.0))

    @pl.when(mode == 3)
    def _():
        o_ref[...] = jnp.zeros_like(o_ref)


def _pack_w_in(w_in):
    depth, K, n_in = w_in.shape
    w_t = jnp.swapaxes(w_in, 1, 2)
    src, mode = _pack_plan()
    sub_per_blk = LANE // SUB
    last8 = n_in // SUB - 1
    return pl.pallas_call(
        _pack_kernel,
        grid_spec=pltpu.PrefetchScalarGridSpec(
            num_scalar_prefetch=2,
            grid=(depth, len(src)),
            in_specs=[pl.BlockSpec((None, LANE, K), lambda l, ob, src_ref, mode_ref: (l, src_ref[ob], 0)),
                      pl.BlockSpec((None, SUB, K), lambda l, ob, src_ref, mode_ref:
                                   (l, jnp.minimum((src_ref[ob] + 1) * sub_per_blk, last8), 0))],
            out_specs=pl.BlockSpec((None, K, LANE), lambda l, ob, src_ref, mode_ref: (l, 0, ob))),
        out_shape=jax.ShapeDtypeStruct((depth, K, len(src) * LANE), BF16),
        compiler_params=_cparams(("parallel", "parallel")),
        name="pack_w_in",
    )(jnp.asarray(src, jnp.int32), jnp.asarray(mode, jnp.int32), w_t, w_t)


W_COL_QKV = 3 * D_MODEL
W_COL_P = 3 * D_MODEL + 3 * D_BR


def _layer_weights(lru_gate_w, rwkv_w_up, rwkv_a_up):
    eye = jnp.eye(H_A, dtype=F32)
    wg = jnp.einsum('ghij,hk->ghikj', lru_gate_w, eye).reshape(2, D_BR, D_BR)
    wg = jnp.concatenate([wg[0], wg[1]], axis=1).astype(BF16)
    z = jnp.zeros((LORA, D_BR), F32)
    wup = jnp.concatenate([rwkv_w_up, z], axis=0).astype(BF16)
    aup = jnp.concatenate([z, rwkv_a_up], axis=0).astype(BF16)
    return wg, wup, aup


def kernel(x, pre_norm_w, post_norm_w, w_in, b_merge, conv_w, conv_b, lru_gate_w, lru_gate_b, lru_lambda,
           fox_b_f, rwkv_mu, rwkv_w0, rwkv_w_up, rwkv_a0, rwkv_a_up, rwkv_k_k, rwkv_k_a, rwkv_r_k,
           rwkv_ln_w, rwkv_ln_b, w_branch, w_out):
    B, S, _ = x.shape
    assert B == 1 and S % CHUNK == 0
    depth = w_in.shape[0]
    w_all = _pack_w_in(w_in)
    xs = x.reshape(S, D_MODEL)
    ones_p = jnp.ones((1, P_COLS), F32)
    qscale = jnp.concatenate([jnp.full((1, D_BR), DH_B ** -0.5 * LOG2E, F32), jnp.ones((1, 2 * D_BR), F32)], axis=1)
    row = lambda a: a.reshape(1, -1)
    for l in range(depth):
        wg, wup, aup = _layer_weights(lru_gate_w[l], rwkv_w_up[l], rwkv_a_up[l])
        h = _rmsnorm_bf16(xs, pre_norm_w[l])
        P = _proj(h, w_all, l, W_COL_P, ones_p, F32, 1536)
        qkv = _proj(h, w_all, l, W_COL_QKV, qscale, BF16, 1024)
        ya = _rglru(P, conv_w[l], row(conv_b[l]), wg, row(lru_gate_b[l]), row(lru_lambda[l]))
        bf_pad = jnp.concatenate([fox_b_f[l], jnp.zeros((LANE - H_B,), F32)]).reshape(1, LANE)
        kx, cs = _fox_c(P, bf_pad)
        yb = _fox(qkv, kx, cs, P)
        mu = rwkv_mu[l]
        mu3 = mu[:3 * D_BR].reshape(3, D_BR)
        muwa = mu[3 * D_BR:].reshape(1, LANE)
        mt, nt, rh, yl, bonus = _rwkv_chunks(P, mu3, muwa, row(rwkv_w0[l]), wup, row(rwkv_a0[l]), aup,
                                             row(rwkv_k_k[l]), row(rwkv_k_a[l]), row(rwkv_r_k[l]))
        yc = _rwkv_seq(mt, nt, rh, yl, bonus, P, row(rwkv_ln_w[l]), row(rwkv_ln_b[l]))
        m = _merge(h, w_all, l, b_merge[l], ya, yb, yc, w_branch[l].astype(BF16))
        xs = _out_proj(m, w_out[l].astype(BF16), xs, post_norm_w[l])
    return xs.reshape(B, S, D_MODEL)
```

```python
import functools

import jax
import jax.numpy as jnp
from jax import lax
from jax.experimental import pallas as pl
from jax.experimental.pallas import tpu as pltpu

F32 = jnp.float32
BF16 = jnp.bfloat16

D_MODEL = 2048
D_BR = 1024
H_A, BW_A = 16, 64
CONV_W = 4
LRU_C = 8.0
H_B, DH_B = 8, 128
H_C, N_C = 16, 64
LORA = 64
NORM_EPS = 1e-6
GN_EPS = 64e-5
LANE = 128
SUB = 8
CHUNK = 64
N_PAIR = H_C // 2
RW_NCH = 2
NEG = -1e30
LOG2E = 1.4426950408889634
FOX_Q = 1024
FOX_T = 1024
FOX_NB = 2
FOX_VMEM = 56 * 1024 * 1024

PC_XA, PC_GA, PC_GB, PC_GC, PC_CR, PC_CK, PC_CV = 0, 8, 16, 24, 32, 40, 48
PC_WA, PC_FL = 56, 57
P_COLS = 60 * LANE

VMEM_LIMIT = 48 * 1024 * 1024


def _cparams(sem):
    return pltpu.CompilerParams(dimension_semantics=sem, vmem_limit_bytes=VMEM_LIMIT)


def _sigmoid(x):
    return 0.5 * jnp.tanh(0.5 * x) + 0.5


def _silu(x):
    return x * _sigmoid(x)


def _softplus(z):
    return jnp.maximum(z, 0.0) + jnp.log1p(jnp.exp(-jnp.abs(z)))


def _split3(x):
    hi = x.astype(BF16)
    r1 = x - hi.astype(F32)
    mid = r1.astype(BF16)
    return hi, mid, (r1 - mid.astype(F32)).astype(BF16)


def _tri_cumsum(lower_mask, x):
    tri = lower_mask.astype(BF16)
    hi, mid, lo = _split3(x)
    return (jnp.dot(tri, hi, preferred_element_type=F32) + jnp.dot(tri, mid, preferred_element_type=F32)
            + jnp.dot(tri, lo, preferred_element_type=F32))


def _proj_kernel(h_ref, w_ref, s_ref, o_ref):
    acc = jnp.dot(h_ref[...], w_ref[...], preferred_element_type=F32)
    o_ref[...] = (acc * s_ref[...]).astype(o_ref.dtype)


def _proj(h, w_all, layer, col0, colscale, out_dtype, tn):
    S, K = h.shape
    N = colscale.shape[1]
    tm = min(1024, S)
    j0 = col0 // tn
    assert col0 % tn == 0 and N % tn == 0
    return pl.pallas_call(
        _proj_kernel,
        grid=(S // tm, N // tn),
        in_specs=[pl.BlockSpec((tm, K), lambda i, j: (i, 0)),
                  pl.BlockSpec((None, K, tn), lambda i, j: (layer, 0, j0 + j)),
                  pl.BlockSpec((1, tn), lambda i, j: (0, j))],
        out_specs=pl.BlockSpec((tm, tn), lambda i, j: (i, j)),
        out_shape=jax.ShapeDtypeStruct((S, N), out_dtype),
        compiler_params=_cparams(("parallel", "parallel")),
        name="in_proj",
    )(h, w_all, colscale)


def _norm_proj_kernel(x_ref, g_ref, w_ref, s_ref, o_ref, h_ref):
    @pl.when(pl.program_id(1) == 0)
    def _():
        x = x_ref[...]
        y = x * lax.rsqrt(jnp.mean(x * x, axis=-1, keepdims=True) + NORM_EPS)
        h_ref[...] = (y * g_ref[...]).astype(h_ref.dtype)

    acc = jnp.dot(h_ref[...], w_ref[...], preferred_element_type=F32)
    o_ref[...] = (acc * s_ref[...]).astype(o_ref.dtype)


def _norm_proj(x, g, w_all, layer, col0, colscale, out_dtype, tn):
    S, K = x.shape
    N = colscale.shape[1]
    tm = min(1024, S)
    j0 = col0 // tn
    assert col0 % tn == 0 and N % tn == 0
    return pl.pallas_call(
        _norm_proj_kernel,
        grid=(S // tm, N // tn),
        in_specs=[pl.BlockSpec((tm, K), lambda i, j: (i, 0)),
                  pl.BlockSpec((1, K), lambda i, j: (0, 0)),
                  pl.BlockSpec((None, K, tn), lambda i, j: (layer, 0, j0 + j)),
                  pl.BlockSpec((1, tn), lambda i, j: (0, j))],
        out_specs=[pl.BlockSpec((tm, tn), lambda i, j: (i, j)),
                   pl.BlockSpec((tm, K), lambda i, j: (i, 0))],
        out_shape=[jax.ShapeDtypeStruct((S, N), out_dtype),
                   jax.ShapeDtypeStruct((S, K), BF16)],
        compiler_params=_cparams(("parallel", "arbitrary")),
        name="norm_proj",
    )(x, g.reshape(1, K), w_all, colscale)


def _rglru_kernel(xa_ref, ga_ref, cw_ref, cb_ref, wg_ref, gbias_ref, lam_ref, o_ref,
                  xe_ref, hc_ref, a_ref, b_ref, *, T):
    i = pl.program_id(0)

    @pl.when(i == 0)
    def _():
        xe_ref[...] = jnp.zeros_like(xe_ref)
        hc_ref[...] = jnp.zeros_like(hc_ref)

    G = T // SUB
    sub = lax.broadcasted_iota(jnp.int32, (G, SUB, D_BR), 1)

    def rot(v, d):
        return pltpu.roll(v, d, 1)

    x = xa_ref[...].reshape(G, SUB, D_BR)
    x_prev = jnp.concatenate([xe_ref[...].reshape(1, SUB, D_BR), x[:G - 1]], axis=0)
    xe_ref[...] = xa_ref[T - SUB:T, :]
    cw = cw_ref[...]
    xc = cb_ref[...] + x * cw[3:4]
    for d in range(1, CONV_W):
        xc = xc + jnp.where(sub < d, rot(x_prev, d), rot(x, d)) * cw[3 - d:4 - d]
    xc = xc.reshape(T, D_BR)

    gates = jnp.dot(xc.astype(BF16), wg_ref[...], preferred_element_type=F32) + gbias_ref[...]
    r = _sigmoid(gates[:, :D_BR])
    ig = _sigmoid(gates[:, D_BR:])
    log_a = (-LRU_C) * r * _softplus(-lam_ref[...])
    a = jnp.exp(log_a)
    y = -jnp.tanh(log_a) * (1.0 + a * a)
    mult = jnp.where(y > 0.0, y * lax.rsqrt(y), 0.0)
    rows = lax.broadcasted_iota(jnp.int32, (T, D_BR), 0)
    mult = jnp.where((rows + i * T) == 0, 1.0, mult)
    b = (mult * ig * xc).reshape(G, SUB, D_BR)
    a = a.reshape(G, SUB, D_BR)

    for d in (1, 2, 4):
        keep = sub >= d
        b = jnp.where(keep, a * rot(b, d) + b, b)
        a = jnp.where(keep, a * rot(a, d), a)
    a_ref[...] = a.reshape(T, D_BR)
    b_ref[...] = b.reshape(T, D_BR)
    carry = hc_ref[...]
    for g in range(T // SUB):
        sl = slice(g * SUB, (g + 1) * SUB)
        hg = b_ref[sl, :] + a_ref[sl, :] * carry
        b_ref[sl, :] = hg
        carry = jnp.broadcast_to(hg[SUB - 1:SUB, :], (SUB, D_BR))
    hc_ref[...] = carry
    o_ref[...] = (b_ref[...] * _silu(ga_ref[...])).astype(o_ref.dtype)


def _rglru(P, conv_w, conv_b, wg, gbias, lam):
    S = P.shape[0]
    T = min(256, S)
    nb = D_BR // LANE
    return pl.pallas_call(
        functools.partial(_rglru_kernel, T=T),
        grid=(S // T,),
        in_specs=[pl.BlockSpec((T, D_BR), lambda i: (i, PC_XA // nb)),
                  pl.BlockSpec((T, D_BR), lambda i: (i, PC_GA // nb)),
                  pl.BlockSpec((CONV_W, D_BR), lambda i: (0, 0)),
                  pl.BlockSpec((1, D_BR), lambda i: (0, 0)),
                  pl.BlockSpec((D_BR, 2 * D_BR), lambda i: (0, 0)),
                  pl.BlockSpec((1, 2 * D_BR), lambda i: (0, 0)),
                  pl.BlockSpec((1, D_BR), lambda i: (0, 0))],
        out_specs=pl.BlockSpec((T, D_BR), lambda i: (i, 0)),
        out_shape=jax.ShapeDtypeStruct((S, D_BR), BF16),
        scratch_shapes=[pltpu.VMEM((SUB, D_BR), F32),
                        pltpu.VMEM((SUB, D_BR), F32),
                        pltpu.VMEM((T, D_BR), F32),
                        pltpu.VMEM((T, D_BR), F32)],
        compiler_params=_cparams(("arbitrary",)),
        name="rglru",
    )(P, P, conv_w, conv_b, wg, gbias, lam)


def _fox_c_kernel(fl_ref, bf_ref, kx_ref, cs_ref, carry_ref, *, T):
    i = pl.program_id(0)

    @pl.when(i == 0)
    def _():
        carry_ref[...] = jnp.zeros_like(carry_ref)

    z = fl_ref[...] + bf_ref[...]
    lf = (jnp.minimum(z, 0.0) - jnp.log1p(jnp.exp(-jnp.abs(z)))) * LOG2E
    TB = min(256, T)
    rr = lax.broadcasted_iota(jnp.int32, (TB, TB), 0)
    cc = lax.broadcasted_iota(jnp.int32, (TB, TB), 1)
    blocks, off = [], jnp.zeros((1, LANE), F32)
    for b in range(T // TB):
        blk = _tri_cumsum(cc <= rr, lf[b * TB:(b + 1) * TB]) + off
        blocks.append(blk)
        off = blk[TB - 1:TB, :]
    lc = jnp.concatenate(blocks, axis=0)
    carry = carry_ref[...]
    cs_ref[0] = carry
    carry_ref[...] = carry + jnp.broadcast_to(off, (SUB, LANE))
    lane = lax.broadcasted_iota(jnp.int32, (T, LANE), 1)
    hi, mid, lo = [jnp.where(lane < H_B, pc.astype(F32), 0.0) for pc in _split3(-lc)]
    packed = (hi + pltpu.roll(mid, H_B, 1) + pltpu.roll(lo, 2 * H_B, 1)).astype(BF16)
    rs = lax.broadcasted_iota(jnp.int32, (LANE, D_BR), 0)
    cs = lax.broadcasted_iota(jnp.int32, (LANE, D_BR), 1)
    sel = ((cs == (rs % H_B) * LANE + rs // H_B) & (rs < 3 * H_B)).astype(BF16)
    kx_ref[...] = jnp.dot(packed, sel, preferred_element_type=F32).astype(BF16)


def _fox_c(P, bf_pad):
    S = P.shape[0]
    T = min(FOX_T, FOX_Q, S // 2)
    return pl.pallas_call(
        functools.partial(_fox_c_kernel, T=T),
        grid=(S // T,),
        in_specs=[pl.BlockSpec((T, LANE), lambda i: (i, PC_FL)),
                  pl.BlockSpec((1, LANE), lambda i: (0, 0))],
        out_specs=[pl.BlockSpec((T, D_BR), lambda i: (i, 0)),
                   pl.BlockSpec((1, SUB, LANE), lambda i: (i, 0, 0))],
        out_shape=[jax.ShapeDtypeStruct((S, D_BR), BF16),
                   jax.ShapeDtypeStruct((S // T, SUB, LANE), F32)],
        scratch_shapes=[pltpu.VMEM((SUB, LANE), F32)],
        compiler_params=_cparams(("arbitrary",)),
        name="fox_cumgate",
    )(P, bf_pad)


def _fox_kernel(cs_ref, q_ref, k_ref, kx_ref, v_ref, gb_ref, o_ref, sh_ref, dp_ref, acc_ref, *, Q, T, NB):
    h = pl.program_id(0)
    i = pl.program_id(1)
    assert Q == T and NB == 2
    lane = lax.broadcasted_iota(jnp.int32, (Q, DH_B), 1)
    qa = jnp.concatenate([q_ref[...], jnp.where(lane < 3, 1.0, 0.0).astype(BF16)], axis=1)
    ones_v = jnp.ones((T, DH_B), BF16)
    c_i = cs_ref[h, i]

    def chunk(j):
        ks = pl.multiple_of(j * T, T)
        kc = jnp.concatenate([k_ref[pl.ds(ks, T), :], kx_ref[pl.ds(ks, T), :]], axis=1)
        vc = jnp.concatenate([v_ref[pl.ds(ks, T), :], ones_v], axis=1)
        return kc, vc, c_i - cs_ref[h, j]

    def qk(kc):
        return lax.dot_general(qa, kc, (((1,), (1,)), ((), ())), preferred_element_type=F32)

    def sm_pv(carry, s, vc, d):
        sh, d_prev, acc = carry
        a = sh + (d_prev - d)
        sh_new = jnp.maximum(a, jnp.max(s, axis=-1, keepdims=True))
        p = jnp.exp2(s - sh_new)
        acc = jnp.exp2(a - sh_new) * acc + jnp.dot(p.astype(BF16), vc, preferred_element_type=F32)
        return sh_new, d, acc

    def chunks(j0, n, carry, diag=False):
        cks = [chunk(j0 + c) for c in range(n)]
        s_next = qk(cks[0][0])
        for c in range(n):
            s = s_next
            if c + 1 < n:
                s_next = qk(cks[c + 1][0])
            elif diag:
                rows = lax.broadcasted_iota(jnp.int32, (Q, T), 0)
                cols = lax.broadcasted_iota(jnp.int32, (Q, T), 1)
                s = jnp.where(cols <= rows, s, NEG)
            carry = sm_pv(carry, s, cks[c][1], cks[c][2])
        return carry

    carry = (jnp.full((Q, 1), NEG, F32), jnp.float32(0.0), jnp.zeros((Q, 2 * DH_B), F32))
    n = i + 1
    tail = jnp.where(n % 2 == 0, 2, jnp.where(n == 1, 1, 3))
    sh, d_prev, acc = lax.fori_loop(0, (n - tail) // 2, lambda jj, c: chunks(2 * jj, 2, c), carry)
    sh_ref[...] = sh
    dp_ref[0] = d_prev
    acc_ref[...] = acc
    for t in (1, 2, 3):
        @pl.when(tail == t)
        def _():
            acc_ref[...] = chunks(n - t, t, (sh_ref[...], dp_ref[0], acc_ref[...]), diag=True)[2]
    acc = acc_ref[...]
    o_ref[...] = (acc[:, :DH_B] / acc[:, DH_B:] * _silu(gb_ref[...])).astype(o_ref.dtype)


def _fox(qkv, kx, cs, P):
    S = qkv.shape[0]
    Q = min(FOX_Q, S // 2)
    T = min(FOX_T, Q)
    cst = cs[:, 0, :H_B].T
    kv = lambda col0: pl.BlockSpec((S, DH_B), lambda h, i: (0, col0 + h))
    return pl.pallas_call(
        functools.partial(_fox_kernel, Q=Q, T=T, NB=FOX_NB),
        grid=(H_B, S // Q),
        in_specs=[pl.BlockSpec(memory_space=pltpu.SMEM),
                  pl.BlockSpec((Q, DH_B), lambda h, i: (i, h)),
                  kv(H_B), kv(0), kv(2 * H_B),
                  pl.BlockSpec((Q, DH_B), lambda h, i: (i, PC_GB + h))],
        out_specs=pl.BlockSpec((Q, DH_B), lambda h, i: (i, h)),
        out_shape=jax.ShapeDtypeStruct((S, D_BR), BF16),
        scratch_shapes=[pltpu.VMEM((Q, 1), F32), pltpu.SMEM((1,), F32),
                        pltpu.VMEM((Q, 2 * DH_B), F32)],
        compiler_params=pltpu.CompilerParams(dimension_semantics=("parallel", "parallel"),
                                             vmem_limit_bytes=FOX_VMEM),
        name="fox_attn",
    )(cst, qkv, qkv, kx, qkv, P)


def _head_sum(x):
    rr = lax.broadcasted_iota(jnp.int32, (LANE, LANE), 0) // N_C
    cc = lax.broadcasted_iota(jnp.int32, (LANE, LANE), 1) // N_C
    ones_bd = (rr == cc).astype(BF16)
    hi = x.astype(BF16)
    lo = (x - hi.astype(F32)).astype(BF16)
    outs = []
    for p in range(N_PAIR):
        sl = slice(p * LANE, (p + 1) * LANE)
        outs.append(jnp.dot(hi[:, sl], ones_bd, preferred_element_type=F32)
                    + jnp.dot(lo[:, sl], ones_bd, preferred_element_type=F32))
    return jnp.concatenate(outs, axis=1)


def _rwkv_tokens(cr_ref, ck_ref, cv_ref, wa_ref, pr_ref, pk_ref, pv_ref, pwa_ref,
                 mu_ref, muwa_ref, w0_ref, wup_ref, a0_ref, aup_ref, kk_ref, ka_ref):
    live = (pl.program_id(0) > 0).astype(F32)

    def shift_mix(cur_ref, prev_ref, mu):
        cur = cur_ref[...]
        rows = lax.broadcasted_iota(jnp.int32, cur.shape, 0)
        first = jnp.broadcast_to(prev_ref[SUB - 1:SUB, :] * live, cur.shape)
        prev = jnp.where(rows == 0, first, pltpu.roll(cur, 1, 0))
        return cur + (prev - cur) * mu

    r = shift_mix(cr_ref, pr_ref, mu_ref[0:1, :])
    kraw = shift_mix(ck_ref, pk_ref, mu_ref[1:2, :])
    v = shift_mix(cv_ref, pv_ref, mu_ref[2:3, :])
    wa = shift_mix(wa_ref, pwa_ref, muwa_ref[...])
    zw = w0_ref[...] + jnp.dot(jnp.tanh(wa).astype(BF16), wup_ref[...], preferred_element_type=F32)
    za = a0_ref[...] + jnp.dot(wa.astype(BF16), aup_ref[...], preferred_element_type=F32)
    g = (-0.6065306597126334) * _sigmoid(zw)
    a = _sigmoid(za)
    kk = kraw * kk_ref[...]
    ss = _head_sum(kk * kk)
    kkn = kk * lax.rsqrt(jnp.maximum(ss, 1e-24))
    return r, kraw * (1.0 + (a - 1.0) * ka_ref[...]), v, kkn, kkn * a, g


def _mm(a, b):
    return jnp.dot(a.astype(BF16), b.astype(BF16), preferred_element_type=F32)


def _mm_nt(a, b):
    return lax.dot_general(a.astype(BF16), b.astype(BF16), (((1,), (1,)), ((), ())),
                           preferred_element_type=F32)


def _mm_tn(a, b):
    return lax.dot_general(a.astype(BF16), b.astype(BF16), (((0,), (0,)), ((), ())),
                           preferred_element_type=F32)


def _rwkv_chunk_kernel(*refs, NCH):
    rk_ref, mt_ref, nt_ref, rh_ref, yl_ref, bonus_ref = refs[16:]
    L = CHUNK
    rv_a, kv_a, vm_a, kkn_a, bv_a, g_a = _rwkv_tokens(*refs[:16])
    bonus_ref[...] = (_head_sum(rv_a * kv_a * rk_ref[...]) * vm_a).astype(bonus_ref.dtype)
    rr = lax.broadcasted_iota(jnp.int32, (L, L), 0)
    cc = lax.broadcasted_iota(jnp.int32, (L, L), 1)
    ops = []
    for ch in range(NCH):
        rows = slice(ch * L, (ch + 1) * L)
        rv, kv, vm, bv, g = rv_a[rows], kv_a[rows], vm_a[rows], bv_a[rows], g_a[rows]
        G = _tri_cumsum(cc <= rr, g)
        Ge = G - g
        Gm = G[L // 2 - 1:L // 2, :]
        GL = G[L - 1:L, :]
        na = -kkn_a[rows]
        e_b = jnp.exp(Gm - G)
        e_l = jnp.exp(GL - G)
        ops.append(dict(al=na * jnp.exp(Ge), al_m=na * jnp.exp(Ge - Gm), be_m=bv * e_b, ka_m=kv * e_b,
                        rho=rv * jnp.exp(G), rho_m=rv * jnp.exp(G - Gm), b_hat=bv * e_l, k_hat=kv * e_l,
                        d_row=jnp.exp(GL), vm=vm))

    lo1 = lax.broadcasted_iota(jnp.int32, (L, LANE), 1) < N_C
    r4 = lax.broadcasted_iota(jnp.int32, (4 * L, LANE), 0)
    c4 = lax.broadcasted_iota(jnp.int32, (4 * L, LANE), 1) & (N_C - 1)
    tri4 = c4 < (r4 & (L - 1)) + jnp.where(r4 < 2 * L, 0, 1)
    rb = lax.broadcasted_iota(jnp.int32, (LANE, LANE), 0)
    cb = lax.broadcasted_iota(jnp.int32, (LANE, LANE), 1)
    bd = (rb // N_C) == (cb // N_C)
    eye = rb == cb
    zeros_l = jnp.zeros((L, LANE), BF16)

    units = [(ch, p) for ch in range(NCH) for p in range(N_PAIR)]
    nu = len(units)
    heads = [(n, hh) for n in range(nu) for hh in range(2)]
    sls = [slice(p * LANE, (p + 1) * LANE) for p in range(N_PAIR)]
    get = lambda name, u: ops[u[0]][name][:, sls[u[1]]]
    vms = [get('vm', u).astype(BF16) for u in units]
    a_cat = []
    for u in units:
        alm_p, rhm_p = get('al_m', u), get('rho_m', u)
        lhs4 = jnp.concatenate([jnp.where(lo1, alm_p, 0.0), jnp.where(lo1, 0.0, alm_p),
                                jnp.where(lo1, rhm_p, 0.0), jnp.where(lo1, 0.0, rhm_p)], axis=0)
        rhs2 = jnp.concatenate([get('be_m', u), get('ka_m', u)], axis=0)
        a_cat.append(jnp.where(tri4, _mm_nt(lhs4, rhs2), 0.0).astype(BF16))
    zv = [jnp.concatenate([zeros_l, v], axis=0) for v in vms]
    akv = [_mm(jnp.where(lo1, 0.0, a_cat[n][hh * L:(hh + 1) * L]), zv[n]) for n, hh in heads]
    xs, apow = [], []
    for n, u in enumerate(units):
        al_p = get('al', u)
        akv_p = jnp.where(lo1, akv[2 * n], akv[2 * n + 1])
        xs.append(jnp.where(lo1, al_p, pltpu.roll(akv_p, N_C, 1)))
        xs.append(jnp.where(lo1, pltpu.roll(al_p, N_C, 1), akv_p))
        for hh in range(2):
            apow.append(jnp.where(lo1, a_cat[n][hh * L:(hh + 1) * L], 0.0).astype(BF16))
    for step in range(6):
        if step < 5:
            res = [_mm(a[:, :N_C], jnp.concatenate([x.astype(BF16), a], axis=1)) for a, x in zip(apow, xs)]
            apow = [r[:, LANE:].astype(BF16) for r in res]
        else:
            res = [_mm(a[:, :N_C], x) for a, x in zip(apow, xs)]
        xs = [x + r[:, :LANE] for x, r in zip(xs, res)]
    w_pq = []
    for n in range(nu):
        x0, x1 = xs[2 * n], xs[2 * n + 1]
        p_pair = jnp.where(lo1, x0, pltpu.roll(x1, N_C, 1)).astype(BF16)
        q_pair = jnp.where(lo1, pltpu.roll(x0, N_C, 1), x1).astype(BF16)
        w_pq.append(jnp.concatenate([jnp.concatenate([p_pair, q_pair], axis=1),
                                     jnp.concatenate([zeros_l, vms[n]], axis=1)], axis=0))
    rbs = [_mm(a_cat[n][(2 + hh) * L:(3 + hh) * L], w_pq[n]) for n, hh in heads]
    gs = [_mm_tn(jnp.concatenate([get('b_hat', u), get('k_hat', u)], axis=0), w_pq[n])
          for n, u in enumerate(units)]
    for n, (ch, p) in enumerate(units):
        sl = sls[p]
        rows = slice(ch * L, (ch + 1) * L)
        r0, r1 = rbs[2 * n], rbs[2 * n + 1]
        rh_ref[rows, sl] = (get('rho', (ch, p)) + jnp.where(lo1, r0[:, :LANE], r1[:, :LANE])).astype(rh_ref.dtype)
        yl_ref[rows, sl] = jnp.where(lo1, r0[:, LANE:], r1[:, LANE:]).astype(yl_ref.dtype)
        d_p = jnp.broadcast_to(ops[ch]['d_row'][:, sl], (LANE, LANE))
        mt_ref[ch, p] = (jnp.where(eye, d_p, 0.0) + jnp.where(bd, gs[n][:, :LANE], 0.0)).astype(mt_ref.dtype)
        nt_ref[ch, p] = jnp.where(bd, gs[n][:, LANE:], 0.0)


def _rwkv_chunks(P, mu3, muwa, w0, wup, a0, aup, k_k, k_a, r_k):
    S = P.shape[0]
    nc = S // CHUNK
    nb = D_BR // LANE
    T = RW_NCH * CHUNK
    prev = lambda c: jnp.maximum(c * (T // SUB) - 1, 0)
    row = lambda n: pl.BlockSpec((1, n), lambda c: (0, 0))
    big = pl.BlockSpec((T, D_BR), lambda c: (c, 0))
    mat = pl.BlockSpec((RW_NCH, N_PAIR, LANE, LANE), lambda c: (c, 0, 0, 0))
    return pl.pallas_call(
        functools.partial(_rwkv_chunk_kernel, NCH=RW_NCH),
        grid=(nc // RW_NCH,),
        in_specs=[pl.BlockSpec((T, D_BR), lambda c: (c, PC_CR // nb)),
                  pl.BlockSpec((T, D_BR), lambda c: (c, PC_CK // nb)),
                  pl.BlockSpec((T, D_BR), lambda c: (c, PC_CV // nb)),
                  pl.BlockSpec((T, LANE), lambda c: (c, PC_WA)),
                  pl.BlockSpec((SUB, D_BR), lambda c: (prev(c), PC_CR // nb)),
                  pl.BlockSpec((SUB, D_BR), lambda c: (prev(c), PC_CK // nb)),
                  pl.BlockSpec((SUB, D_BR), lambda c: (prev(c), PC_CV // nb)),
                  pl.BlockSpec((SUB, LANE), lambda c: (prev(c), PC_WA)),
                  pl.BlockSpec((3, D_BR), lambda c: (0, 0)),
                  row(LANE), row(D_BR),
                  pl.BlockSpec((LANE, D_BR), lambda c: (0, 0)),
                  row(D_BR),
                  pl.BlockSpec((LANE, D_BR), lambda c: (0, 0)),
                  row(D_BR), row(D_BR), row(D_BR)],
        out_specs=[mat, mat, big, big, big],
        out_shape=[jax.ShapeDtypeStruct((nc, N_PAIR, LANE, LANE), BF16),
                   jax.ShapeDtypeStruct((nc, N_PAIR, LANE, LANE), F32),
                   jax.ShapeDtypeStruct((S, D_BR), BF16),
                   jax.ShapeDtypeStruct((S, D_BR), BF16),
                   jax.ShapeDtypeStruct((S, D_BR), BF16)],
        compiler_params=_cparams(("parallel",)),
        name="rwkv_chunks",
    )(P, P, P, P, P, P, P, P, mu3, muwa, w0, wup, a0, aup, k_k, k_a, r_k)


def _rwkv_seq_kernel(mt_ref, nt_ref, rh_ref, yl_ref, bonus_ref, gc_ref,
                     lnw_ref, lnb_ref, o_ref, z_ref, y_ref, *, NCH):
    i = pl.program_id(0)

    @pl.when(i == 0)
    def _():
        z_ref[...] = jnp.zeros_like(z_ref)

    zs = [z_ref[p] for p in range(N_PAIR)]
    for c in range(NCH):
        rows = slice(c * CHUNK, (c + 1) * CHUNK)
        zb = [z.astype(BF16) for z in zs]
        zs = [jnp.dot(mt_ref[c, p], zb[p], preferred_element_type=F32) + nt_ref[c, p]
              for p in range(N_PAIR)]
        for p in range(N_PAIR):
            sl = slice(p * LANE, (p + 1) * LANE)
            y_ref[rows, sl] = jnp.dot(rh_ref[rows, sl], zb[p], preferred_element_type=F32) + yl_ref[rows, sl]
    for p in range(N_PAIR):
        z_ref[p] = zs[p]

    y = y_ref[...]
    inv_n = 1.0 / N_C
    mu = _head_sum(y) * inv_n
    yc = y - mu
    var = _head_sum(yc * yc) * inv_n
    yn = yc * lax.rsqrt(var + GN_EPS) * lnw_ref[...] + lnb_ref[...]
    o_ref[...] = ((yn + bonus_ref[...]) * _silu(gc_ref[...])).astype(o_ref.dtype)


def _rwkv_seq(mt, nt, rh, yl, bonus, P, ln_w, ln_b):
    S = rh.shape[0]
    T = min(256, S)
    nch = T // CHUNK
    nb = D_BR // LANE
    big = pl.BlockSpec((T, D_BR), lambda i: (i, 0))
    mat = pl.BlockSpec((nch, N_PAIR, LANE, LANE), lambda i: (i, 0, 0, 0))
    row = pl.BlockSpec((1, D_BR), lambda i: (0, 0))
    return pl.pallas_call(
        functools.partial(_rwkv_seq_kernel, NCH=nch),
        grid=(S // T,),
        in_specs=[mat, mat, big, big, big,
                  pl.BlockSpec((T, D_BR), lambda i: (i, PC_GC // nb)),
                  row, row],
        out_specs=big,
        out_shape=jax.ShapeDtypeStruct((S, D_BR), BF16),
        scratch_shapes=[pltpu.VMEM((N_PAIR, LANE, LANE), F32),
                        pltpu.VMEM((T, D_BR), F32)],
        compiler_params=_cparams(("arbitrary",)),
        name="rwkv_seq",
    )(mt, nt, rh, yl, bonus, P, ln_w, ln_b)


def _merge_kernel(h_ref, wg0_ref, wg1_ref, wg2_ref, bm_ref, ya_ref, yb_ref, yc_ref,
                  wb0_ref, wb1_ref, wb2_ref, o_ref):
    h = h_ref[...]
    bm = bm_ref[...]
    acc = None
    for n, (wg_ref, y_ref, wb_ref) in enumerate(((wg0_ref, ya_ref, wb0_ref),
                                                  (wg1_ref, yb_ref, wb1_ref),
                                                  (wg2_ref, yc_ref, wb2_ref))):
        gate = _sigmoid(jnp.dot(h, wg_ref[...], preferred_element_type=F32) + bm[n:n + 1, :])
        term = gate * jnp.dot(y_ref[...], wb_ref[...], preferred_element_type=F32)
        acc = term if acc is None else acc + term
    o_ref[...] = acc.astype(o_ref.dtype)


def _merge(h, w_all, layer, b_merge, ya, yb, yc, w_br):
    S = h.shape[0]
    tm = min(1024, S)
    tn = 512
    nj = D_MODEL // tn
    ysp = pl.BlockSpec((tm, D_BR), lambda i, j: (i, 0))
    wgate = lambda n: pl.BlockSpec((None, D_MODEL, tn), lambda i, j: (layer, 0, n * nj + j))
    return pl.pallas_call(
        _merge_kernel,
        grid=(S // tm, nj),
        in_specs=[pl.BlockSpec((tm, D_MODEL), lambda i, j: (i, 0)),
                  wgate(0), wgate(1), wgate(2),
                  pl.BlockSpec((3, tn), lambda i, j: (0, j)),
                  ysp, ysp, ysp,
                  pl.BlockSpec((None, D_BR, tn), lambda i, j: (0, 0, j)),
                  pl.BlockSpec((None, D_BR, tn), lambda i, j: (1, 0, j)),
                  pl.BlockSpec((None, D_BR, tn), lambda i, j: (2, 0, j))],
        out_specs=pl.BlockSpec((tm, tn), lambda i, j: (i, j)),
        out_shape=jax.ShapeDtypeStruct((S, D_MODEL), BF16),
        compiler_params=pltpu.CompilerParams(dimension_semantics=("parallel", "parallel"),
                                             vmem_limit_bytes=FOX_VMEM),
        name="merge",
    )(h, w_all, w_all, w_all, b_merge, ya, yb, yc, w_br, w_br, w_br)


def _out_kernel(m_ref, w_ref, x_ref, g_ref, o_ref):
    y = jnp.dot(m_ref[...], w_ref[...], preferred_element_type=F32)
    yn = y * lax.rsqrt(jnp.mean(y * y, axis=-1, keepdims=True) + NORM_EPS)
    o_ref[...] = x_ref[...] + yn * g_ref[...]


def _out_proj(m, w_out, x, g):
    S = m.shape[0]
    tm = min(512, S)
    blk = pl.BlockSpec((tm, D_MODEL), lambda i: (i, 0))
    return pl.pallas_call(
        _out_kernel,
        grid=(S // tm,),
        in_specs=[blk,
                  pl.BlockSpec((D_MODEL, D_MODEL), lambda i: (0, 0)),
                  blk,
                  pl.BlockSpec((1, D_MODEL), lambda i: (0, 0))],
        out_specs=blk,
        out_shape=jax.ShapeDtypeStruct((S, D_MODEL), F32),
        compiler_params=_cparams(("parallel",)),
        name="out_proj",
    )(m, w_out, x, g.reshape(1, D_MODEL))


def _pack_plan():
    xa, qkv, fl = 0, 2 * D_BR, 5 * D_BR
    gb = fl + H_B
    pc = gb + D_BR
    gc = pc + 3 * D_BR + 2 * LORA
    mg = gc + D_BR
    groups = [(mg, 3 * D_MODEL), (qkv, 3 * D_BR), (xa, 2 * D_BR), (gb, D_BR), (gc, D_BR),
              (pc, 3 * D_BR + 2 * LORA)]
    src, mode = [], []
    for col0, width in groups:
        for j in range(width // LANE):
            c = col0 + j * LANE
            src.append(c // LANE)
            mode.append(0 if c % LANE == 0 else 1)
            assert c % LANE in (0, H_B)
    src.append(fl // LANE)
    mode.append(2)
    n_pad = P_COLS // LANE - (len(src) - (3 * D_MODEL + 3 * D_BR) // LANE)
    src += [0] * n_pad
    mode += [3] * n_pad
    return src, mode


PACK_NB = 4


def _pack_kernel(src_ref, mode_ref, *refs):
    a_refs, b_refs, o_ref = refs[:PACK_NB], refs[PACK_NB:2 * PACK_NB], refs[2 * PACK_NB]
    for n in range(PACK_NB):
        mode = mode_ref[pl.program_id(1) * PACK_NB + n]
        a_ref, b_ref = a_refs[n], b_refs[n]
        cols = slice(n * LANE, (n + 1) * LANE)

        def emit(t, cols=cols):
            o_ref[:, cols] = t.T.astype(o_ref.dtype)

        @pl.when(mode == 0)
        def _():
            emit(a_ref[...])

        @pl.when(mode == 1)
        def _():
            emit(jnp.concatenate([a_ref[SUB:, :], b_ref[...]], axis=0))

        @pl.when(mode == 2)
        def _():
            a = a_ref[...]
            emit(jnp.where(lax.broadcasted_iota(jnp.int32, a.shape, 0) < H_B, a, 0.0))

        @pl.when(mode == 3)
        def _():
            o_ref[:, cols] = jnp.zeros((o_ref.shape[0], LANE), o_ref.dtype)


def _pack_w_in(w_in):
    depth, K, n_in = w_in.shape
    w_t = jnp.swapaxes(w_in, 1, 2)
    src, mode = _pack_plan()
    assert len(src) % PACK_NB == 0
    sub_per_blk = LANE // SUB
    last8 = n_in // SUB - 1

    def a_spec(n):
        return pl.BlockSpec((None, LANE, K), lambda l, ob, src_ref, mode_ref: (l, src_ref[ob * PACK_NB + n], 0))

    def b_spec(n):
        return pl.BlockSpec((None, SUB, K), lambda l, ob, src_ref, mode_ref:
                            (l, jnp.minimum((src_ref[ob * PACK_NB + n] + 1) * sub_per_blk, last8), 0))

    return pl.pallas_call(
        _pack_kernel,
        grid_spec=pltpu.PrefetchScalarGridSpec(
            num_scalar_prefetch=2,
            grid=(depth, len(src) // PACK_NB),
            in_specs=[a_spec(n) for n in range(PACK_NB)] + [b_spec(n) for n in range(PACK_NB)],
            out_specs=pl.BlockSpec((None, K, PACK_NB * LANE), lambda l, ob, src_ref, mode_ref: (l, 0, ob))),
        out_shape=jax.ShapeDtypeStruct((depth, K, len(src) * LANE), BF16),
        compiler_params=_cparams(("parallel", "parallel")),
        name="pack_w_in",
    )(jnp.asarray(src, jnp.int32), jnp.asarray(mode, jnp.int32), *([w_t] * (2 * PACK_NB)))


W_COL_QKV = 3 * D_MODEL
W_COL_P = 3 * D_MODEL + 3 * D_BR


def _layer_weights(lru_gate_w, rwkv_w_up, rwkv_a_up):
    eye = jnp.eye(H_A, dtype=F32)
    wg = jnp.einsum('ghij,hk->ghikj', lru_gate_w, eye).reshape(2, D_BR, D_BR)
    wg = jnp.concatenate([wg[0], wg[1]], axis=1).astype(BF16)
    z = jnp.zeros((LORA, D_BR), F32)
    wup = jnp.concatenate([rwkv_w_up, z], axis=0).astype(BF16)
    aup = jnp.concatenate([z, rwkv_a_up], axis=0).astype(BF16)
    return wg, wup, aup


def kernel(x, pre_norm_w, post_norm_w, w_in, b_merge, conv_w, conv_b, lru_gate_w, lru_gate_b, lru_lambda,
           fox_b_f, rwkv_mu, rwkv_w0, rwkv_w_up, rwkv_a0, rwkv_a_up, rwkv_k_k, rwkv_k_a, rwkv_r_k,
           rwkv_ln_w, rwkv_ln_b, w_branch, w_out):
    B, S, _ = x.shape
    assert B == 1 and S % CHUNK == 0
    depth = w_in.shape[0]
    w_all = _pack_w_in(w_in)
    xs = x.reshape(S, D_MODEL)
    ones_p = jnp.ones((1, P_COLS), F32)
    qscale = jnp.concatenate([jnp.full((1, D_BR), DH_B ** -0.5 * LOG2E, F32), jnp.ones((1, 2 * D_BR), F32)], axis=1)
    row = lambda a: a.reshape(1, -1)
    for l in range(depth):
        wg, wup, aup = _layer_weights(lru_gate_w[l], rwkv_w_up[l], rwkv_a_up[l])
        qkv, h = _norm_proj(xs, pre_norm_w[l], w_all, l, W_COL_QKV, qscale, BF16, 1024)
        P = _proj(h, w_all, l, W_COL_P, ones_p, F32, 1536)
        ya = _rglru(P, conv_w[l], row(conv_b[l]), wg, row(lru_gate_b[l]), row(lru_lambda[l]))
        bf_pad = jnp.concatenate([fox_b_f[l], jnp.zeros((LANE - H_B,), F32)]).reshape(1, LANE)
        kx, cs = _fox_c(P, bf_pad)
        yb = _fox(qkv, kx, cs, P)
        mu = rwkv_mu[l]
        mu3 = mu[:3 * D_BR].reshape(3, D_BR)
        muwa = mu[3 * D_BR:].reshape(1, LANE)
        mt, nt, rh, yl, bonus = _rwkv_chunks(P, mu3, muwa, row(rwkv_w0[l]), wup, row(rwkv_a0[l]), aup,
                                             row(rwkv_k_k[l]), row(rwkv_k_a[l]), row(rwkv_r_k[l]))
        yc = _rwkv_seq(mt, nt, rh, yl, bonus, P, row(rwkv_ln_w[l]), row(rwkv_ln_b[l]))
        m = _merge(h, w_all, l, b_merge[l], ya, yb, yc, w_branch[l].astype(BF16))
        xs = _out_proj(m, w_out[l].astype(BF16), xs, post_norm_w[l])
    return xs.reshape(B, S, D_MODEL)
```

```python
import functools

import jax
import jax.numpy as jnp
from jax import lax
from jax.experimental import pallas as pl
from jax.experimental.pallas import tpu as pltpu

F32 = jnp.float32
BF16 = jnp.bfloat16

D_MODEL = 2048
D_BR = 1024
H_A, BW_A = 16, 64
CONV_W = 4
LRU_C = 8.0
H_B, DH_B = 8, 128
H_C, N_C = 16, 64
LORA = 64
NORM_EPS = 1e-6
GN_EPS = 64e-5
LANE = 128
SUB = 8
CHUNK = 64
N_PAIR = H_C // 2
RW_NCH = 2
NEG = -1e30
LOG2E = 1.4426950408889634
FOX_Q = 1024
FOX_T = 1024
FOX_NB = 2
FOX_VMEM = 56 * 1024 * 1024

PC_XA, PC_GA, PC_GB, PC_GC, PC_CR, PC_CK, PC_CV = 0, 8, 16, 24, 32, 40, 48
PC_WA, PC_FL = 56, 57
P_COLS = 60 * LANE

VMEM_LIMIT = 48 * 1024 * 1024


def _cparams(sem):
    return pltpu.CompilerParams(dimension_semantics=sem, vmem_limit_bytes=VMEM_LIMIT)


def _sigmoid(x):
    return 0.5 * jnp.tanh(0.5 * x) + 0.5


def _silu(x):
    return x * _sigmoid(x)


def _softplus(z):
    return jnp.maximum(z, 0.0) + jnp.log1p(jnp.exp(-jnp.abs(z)))


def _split3(x):
    hi = x.astype(BF16)
    r1 = x - hi.astype(F32)
    mid = r1.astype(BF16)
    return hi, mid, (r1 - mid.astype(F32)).astype(BF16)


def _tri_cumsum(lower_mask, x):
    tri = lower_mask.astype(BF16)
    hi, mid, lo = _split3(x)
    return (jnp.dot(tri, hi, preferred_element_type=F32) + jnp.dot(tri, mid, preferred_element_type=F32)
            + jnp.dot(tri, lo, preferred_element_type=F32))


def _proj_kernel(h_ref, w_ref, s_ref, o_ref):
    acc = jnp.dot(h_ref[...], w_ref[...], preferred_element_type=F32)
    o_ref[...] = (acc * s_ref[...]).astype(o_ref.dtype)


def _proj(h, w_all, layer, col0, colscale, out_dtype, tn):
    S, K = h.shape
    N = colscale.shape[1]
    tm = min(1024, S)
    j0 = col0 // tn
    assert col0 % tn == 0 and N % tn == 0
    return pl.pallas_call(
        _proj_kernel,
        grid=(S // tm, N // tn),
        in_specs=[pl.BlockSpec((tm, K), lambda i, j: (i, 0)),
                  pl.BlockSpec((None, K, tn), lambda i, j: (layer, 0, j0 + j)),
                  pl.BlockSpec((1, tn), lambda i, j: (0, j))],
        out_specs=pl.BlockSpec((tm, tn), lambda i, j: (i, j)),
        out_shape=jax.ShapeDtypeStruct((S, N), out_dtype),
        compiler_params=_cparams(("parallel", "parallel")),
        name="in_proj",
    )(h, w_all, colscale)


def _norm_proj_kernel(x_ref, g_ref, w_ref, s_ref, o_ref, h_ref):
    @pl.when(pl.program_id(1) == 0)
    def _():
        x = x_ref[...]
        y = x * lax.rsqrt(jnp.mean(x * x, axis=-1, keepdims=True) + NORM_EPS)
        h_ref[...] = (y * g_ref[...]).astype(h_ref.dtype)

    acc = jnp.dot(h_ref[...], w_ref[...], preferred_element_type=F32)
    o_ref[...] = (acc * s_ref[...]).astype(o_ref.dtype)


def _norm_proj(x, g, w_all, layer, col0, colscale, out_dtype, tn):
    S, K = x.shape
    N = colscale.shape[1]
    tm = min(1024, S)
    j0 = col0 // tn
    assert col0 % tn == 0 and N % tn == 0
    return pl.pallas_call(
        _norm_proj_kernel,
        grid=(S // tm, N // tn),
        in_specs=[pl.BlockSpec((tm, K), lambda i, j: (i, 0)),
                  pl.BlockSpec((1, K), lambda i, j: (0, 0)),
                  pl.BlockSpec((None, K, tn), lambda i, j: (layer, 0, j0 + j)),
                  pl.BlockSpec((1, tn), lambda i, j: (0, j))],
        out_specs=[pl.BlockSpec((tm, tn), lambda i, j: (i, j)),
                   pl.BlockSpec((tm, K), lambda i, j: (i, 0))],
        out_shape=[jax.ShapeDtypeStruct((S, N), out_dtype),
                   jax.ShapeDtypeStruct((S, K), BF16)],
        compiler_params=_cparams(("parallel", "arbitrary")),
        name="norm_proj",
    )(x, g.reshape(1, K), w_all, colscale)


def _rglru_kernel(xa_ref, ga_ref, cw_ref, cb_ref, wg_ref, gbias_ref, lam_ref, o_ref,
                  xe_ref, hc_ref, a_ref, b_ref, *, T):
    i = pl.program_id(0)

    @pl.when(i == 0)
    def _():
        xe_ref[...] = jnp.zeros_like(xe_ref)
        hc_ref[...] = jnp.zeros_like(hc_ref)

    G = T // SUB
    sub = lax.broadcasted_iota(jnp.int32, (G, SUB, D_BR), 1)

    def rot(v, d):
        return pltpu.roll(v, d, 1)

    x = xa_ref[...].reshape(G, SUB, D_BR)
    x_prev = jnp.concatenate([xe_ref[...].reshape(1, SUB, D_BR), x[:G - 1]], axis=0)
    xe_ref[...] = xa_ref[T - SUB:T, :]
    cw = cw_ref[...]
    xc = cb_ref[...] + x * cw[3:4]
    for d in range(1, CONV_W):
        xc = xc + jnp.where(sub < d, rot(x_prev, d), rot(x, d)) * cw[3 - d:4 - d]
    xc = xc.reshape(T, D_BR)

    gates = jnp.dot(xc.astype(BF16), wg_ref[...], preferred_element_type=F32) + gbias_ref[...]
    r = _sigmoid(gates[:, :D_BR])
    ig = _sigmoid(gates[:, D_BR:])
    log_a = (-LRU_C) * r * _softplus(-lam_ref[...])
    a = jnp.exp(log_a)
    y = -jnp.tanh(log_a) * (1.0 + a * a)
    mult = jnp.where(y > 0.0, y * lax.rsqrt(y), 0.0)
    rows = lax.broadcasted_iota(jnp.int32, (T, D_BR), 0)
    mult = jnp.where((rows + i * T) == 0, 1.0, mult)
    b = (mult * ig * xc).reshape(G, SUB, D_BR)
    a = a.reshape(G, SUB, D_BR)

    for d in (1, 2, 4):
        keep = sub >= d
        b = jnp.where(keep, a * rot(b, d) + b, b)
        a = jnp.where(keep, a * rot(a, d), a)
    a_ref[...] = a.reshape(T, D_BR)
    b_ref[...] = b.reshape(T, D_BR)
    carry = hc_ref[...]
    for g in range(T // SUB):
        sl = slice(g * SUB, (g + 1) * SUB)
        hg = b_ref[sl, :] + a_ref[sl, :] * carry
        b_ref[sl, :] = hg
        carry = jnp.broadcast_to(hg[SUB - 1:SUB, :], (SUB, D_BR))
    hc_ref[...] = carry
    o_ref[...] = (b_ref[...] * _silu(ga_ref[...])).astype(o_ref.dtype)


def _rglru(P, conv_w, conv_b, wg, gbias, lam):
    S = P.shape[0]
    T = min(512, S)
    nb = D_BR // LANE
    return pl.pallas_call(
        functools.partial(_rglru_kernel, T=T),
        grid=(S // T,),
        in_specs=[pl.BlockSpec((T, D_BR), lambda i: (i, PC_XA // nb)),
                  pl.BlockSpec((T, D_BR), lambda i: (i, PC_GA // nb)),
                  pl.BlockSpec((CONV_W, D_BR), lambda i: (0, 0)),
                  pl.BlockSpec((1, D_BR), lambda i: (0, 0)),
                  pl.BlockSpec((D_BR, 2 * D_BR), lambda i: (0, 0)),
                  pl.BlockSpec((1, 2 * D_BR), lambda i: (0, 0)),
                  pl.BlockSpec((1, D_BR), lambda i: (0, 0))],
        out_specs=pl.BlockSpec((T, D_BR), lambda i: (i, 0)),
        out_shape=jax.ShapeDtypeStruct((S, D_BR), BF16),
        scratch_shapes=[pltpu.VMEM((SUB, D_BR), F32),
                        pltpu.VMEM((SUB, D_BR), F32),
                        pltpu.VMEM((T, D_BR), F32),
                        pltpu.VMEM((T, D_BR), F32)],
        compiler_params=_cparams(("arbitrary",)),
        name="rglru",
    )(P, P, conv_w, conv_b, wg, gbias, lam)


def _fox_c_kernel(fl_ref, bf_ref, kx_ref, cs_ref, carry_ref, *, T):
    i = pl.program_id(0)

    @pl.when(i == 0)
    def _():
        carry_ref[...] = jnp.zeros_like(carry_ref)

    z = fl_ref[...] + bf_ref[...]
    lf = (jnp.minimum(z, 0.0) - jnp.log1p(jnp.exp(-jnp.abs(z)))) * LOG2E
    TB = min(256, T)
    rr = lax.broadcasted_iota(jnp.int32, (TB, TB), 0)
    cc = lax.broadcasted_iota(jnp.int32, (TB, TB), 1)
    blocks, off = [], jnp.zeros((1, LANE), F32)
    for b in range(T // TB):
        blk = _tri_cumsum(cc <= rr, lf[b * TB:(b + 1) * TB]) + off
        blocks.append(blk)
        off = blk[TB - 1:TB, :]
    lc = jnp.concatenate(blocks, axis=0)
    carry = carry_ref[...]
    cs_ref[0] = carry
    carry_ref[...] = carry + jnp.broadcast_to(off, (SUB, LANE))
    lane = lax.broadcasted_iota(jnp.int32, (T, LANE), 1)
    hi, mid, lo = [jnp.where(lane < H_B, pc.astype(F32), 0.0) for pc in _split3(-lc)]
    packed = (hi + pltpu.roll(mid, H_B, 1) + pltpu.roll(lo, 2 * H_B, 1)).astype(BF16)
    rs = lax.broadcasted_iota(jnp.int32, (LANE, D_BR), 0)
    cs = lax.broadcasted_iota(jnp.int32, (LANE, D_BR), 1)
    sel = ((cs == (rs % H_B) * LANE + rs // H_B) & (rs < 3 * H_B)).astype(BF16)
    kx_ref[...] = jnp.dot(packed, sel, preferred_element_type=F32).astype(BF16)


def _fox_c(P, bf_pad):
    S = P.shape[0]
    T = min(FOX_T, FOX_Q, S // 2)
    return pl.pallas_call(
        functools.partial(_fox_c_kernel, T=T),
        grid=(S // T,),
        in_specs=[pl.BlockSpec((T, LANE), lambda i: (i, PC_FL)),
                  pl.BlockSpec((1, LANE), lambda i: (0, 0))],
        out_specs=[pl.BlockSpec((T, D_BR), lambda i: (i, 0)),
                   pl.BlockSpec((1, SUB, LANE), lambda i: (i, 0, 0))],
        out_shape=[jax.ShapeDtypeStruct((S, D_BR), BF16),
                   jax.ShapeDtypeStruct((S // T, SUB, LANE), F32)],
        scratch_shapes=[pltpu.VMEM((SUB, LANE), F32)],
        compiler_params=_cparams(("arbitrary",)),
        name="fox_cumgate",
    )(P, bf_pad)


def _fox_kernel(cs_ref, q_ref, k_ref, kx_ref, v_ref, gb_ref, o_ref, sh_ref, dp_ref, acc_ref, *, Q, T, NB):
    h = pl.program_id(0)
    i = pl.program_id(1)
    assert Q == T and NB == 2
    lane = lax.broadcasted_iota(jnp.int32, (Q, DH_B), 1)
    qa = jnp.concatenate([q_ref[...], jnp.where(lane < 3, 1.0, 0.0).astype(BF16)], axis=1)
    ones_v = jnp.ones((T, DH_B), BF16)
    c_i = cs_ref[h, i]

    def chunk(j):
        ks = pl.multiple_of(j * T, T)
        kc = jnp.concatenate([k_ref[pl.ds(ks, T), :], kx_ref[pl.ds(ks, T), :]], axis=1)
        vc = jnp.concatenate([v_ref[pl.ds(ks, T), :], ones_v], axis=1)
        return kc, vc, c_i - cs_ref[h, j]

    def qk(kc):
        return lax.dot_general(qa, kc, (((1,), (1,)), ((), ())), preferred_element_type=F32)

    def sm_pv(carry, s, vc, d):
        sh, d_prev, acc = carry
        a = sh + (d_prev - d)
        sh_new = jnp.maximum(a, jnp.max(s, axis=-1, keepdims=True))
        p = jnp.exp2(s - sh_new)
        acc = jnp.exp2(a - sh_new) * acc + jnp.dot(p.astype(BF16), vc, preferred_element_type=F32)
        return sh_new, d, acc

    def chunks(j0, n, carry, diag=False):
        cks = [chunk(j0 + c) for c in range(n)]
        s_next = qk(cks[0][0])
        for c in range(n):
            s = s_next
            if c + 1 < n:
                s_next = qk(cks[c + 1][0])
            elif diag:
                rows = lax.broadcasted_iota(jnp.int32, (Q, T), 0)
                cols = lax.broadcasted_iota(jnp.int32, (Q, T), 1)
                s = jnp.where(cols <= rows, s, NEG)
            carry = sm_pv(carry, s, cks[c][1], cks[c][2])
        return carry

    carry = (jnp.full((Q, 1), NEG, F32), jnp.float32(0.0), jnp.zeros((Q, 2 * DH_B), F32))
    n = i + 1
    tail = jnp.where(n % 2 == 0, 2, jnp.where(n == 1, 1, 3))
    sh, d_prev, acc = lax.fori_loop(0, (n - tail) // 2, lambda jj, c: chunks(2 * jj, 2, c), carry)
    sh_ref[...] = sh
    dp_ref[0] = d_prev
    acc_ref[...] = acc
    for t in (1, 2, 3):
        @pl.when(tail == t)
        def _():
            acc_ref[...] = chunks(n - t, t, (sh_ref[...], dp_ref[0], acc_ref[...]), diag=True)[2]
    acc = acc_ref[...]
    o_ref[...] = (acc[:, :DH_B] / acc[:, DH_B:] * _silu(gb_ref[...])).astype(o_ref.dtype)


def _fox(qkv, kx, cs, P):
    S = qkv.shape[0]
    Q = min(FOX_Q, S // 2)
    T = min(FOX_T, Q)
    cst = cs[:, 0, :H_B].T
    kv = lambda col0: pl.BlockSpec((S, DH_B), lambda h, i: (0, col0 + h))
    return pl.pallas_call(
        functools.partial(_fox_kernel, Q=Q, T=T, NB=FOX_NB),
        grid=(H_B, S // Q),
        in_specs=[pl.BlockSpec(memory_space=pltpu.SMEM),
                  pl.BlockSpec((Q, DH_B), lambda h, i: (i, h)),
                  kv(H_B), kv(0), kv(2 * H_B),
                  pl.BlockSpec((Q, DH_B), lambda h, i: (i, PC_GB + h))],
        out_specs=pl.BlockSpec((Q, DH_B), lambda h, i: (i, h)),
        out_shape=jax.ShapeDtypeStruct((S, D_BR), BF16),
        scratch_shapes=[pltpu.VMEM((Q, 1), F32), pltpu.SMEM((1,), F32),
                        pltpu.VMEM((Q, 2 * DH_B), F32)],
        compiler_params=pltpu.CompilerParams(dimension_semantics=("parallel", "parallel"),
                                             vmem_limit_bytes=FOX_VMEM),
        name="fox_attn",
    )(cst, qkv, qkv, kx, qkv, P)


def _head_sum(x):
    rr = lax.broadcasted_iota(jnp.int32, (LANE, LANE), 0) // N_C
    cc = lax.broadcasted_iota(jnp.int32, (LANE, LANE), 1) // N_C
    ones_bd = (rr == cc).astype(BF16)
    hi = x.astype(BF16)
    lo = (x - hi.astype(F32)).astype(BF16)
    outs = []
    for p in range(N_PAIR):
        sl = slice(p * LANE, (p + 1) * LANE)
        outs.append(jnp.dot(hi[:, sl], ones_bd, preferred_element_type=F32)
                    + jnp.dot(lo[:, sl], ones_bd, preferred_element_type=F32))
    return jnp.concatenate(outs, axis=1)


def _rwkv_tokens(cr_ref, ck_ref, cv_ref, wa_ref, pr_ref, pk_ref, pv_ref, pwa_ref,
                 mu_ref, muwa_ref, w0_ref, wup_ref, a0_ref, aup_ref, kk_ref, ka_ref):
    live = (pl.program_id(0) > 0).astype(F32)

    def shift_mix(cur_ref, prev_ref, mu):
        cur = cur_ref[...]
        rows = lax.broadcasted_iota(jnp.int32, cur.shape, 0)
        first = jnp.broadcast_to(prev_ref[SUB - 1:SUB, :] * live, cur.shape)
        prev = jnp.where(rows == 0, first, pltpu.roll(cur, 1, 0))
        return cur + (prev - cur) * mu

    r = shift_mix(cr_ref, pr_ref, mu_ref[0:1, :])
    kraw = shift_mix(ck_ref, pk_ref, mu_ref[1:2, :])
    v = shift_mix(cv_ref, pv_ref, mu_ref[2:3, :])
    wa = shift_mix(wa_ref, pwa_ref, muwa_ref[...])
    zw = w0_ref[...] + jnp.dot(jnp.tanh(wa).astype(BF16), wup_ref[...], preferred_element_type=F32)
    za = a0_ref[...] + jnp.dot(wa.astype(BF16), aup_ref[...], preferred_element_type=F32)
    g = (-0.6065306597126334) * _sigmoid(zw)
    a = _sigmoid(za)
    kk = kraw * kk_ref[...]
    ss = _head_sum(kk * kk)
    kkn = kk * lax.rsqrt(jnp.maximum(ss, 1e-24))
    return r, kraw * (1.0 + (a - 1.0) * ka_ref[...]), v, kkn, kkn * a, g


def _mm(a, b):
    return jnp.dot(a.astype(BF16), b.astype(BF16), preferred_element_type=F32)


def _mm_nt(a, b):
    return lax.dot_general(a.astype(BF16), b.astype(BF16), (((1,), (1,)), ((), ())),
                           preferred_element_type=F32)


def _mm_tn(a, b):
    return lax.dot_general(a.astype(BF16), b.astype(BF16), (((0,), (0,)), ((), ())),
                           preferred_element_type=F32)


def _rwkv_chunk_kernel(*refs, NCH):
    rk_ref, mt_ref, nt_ref, rh_ref, yl_ref, bonus_ref = refs[16:]
    L = CHUNK
    rv_a, kv_a, vm_a, kkn_a, bv_a, g_a = _rwkv_tokens(*refs[:16])
    bonus_ref[...] = (_head_sum(rv_a * kv_a * rk_ref[...]) * vm_a).astype(bonus_ref.dtype)
    rr = lax.broadcasted_iota(jnp.int32, (L, L), 0)
    cc = lax.broadcasted_iota(jnp.int32, (L, L), 1)
    ops = []
    for ch in range(NCH):
        rows = slice(ch * L, (ch + 1) * L)
        rv, kv, vm, bv, g = rv_a[rows], kv_a[rows], vm_a[rows], bv_a[rows], g_a[rows]
        G = _tri_cumsum(cc <= rr, g)
        Ge = G - g
        Gm = G[L // 2 - 1:L // 2, :]
        GL = G[L - 1:L, :]
        na = -kkn_a[rows]
        e_b = jnp.exp(Gm - G)
        e_l = jnp.exp(GL - G)
        ops.append(dict(al=na * jnp.exp(Ge), al_m=na * jnp.exp(Ge - Gm), be_m=bv * e_b, ka_m=kv * e_b,
                        rho=rv * jnp.exp(G), rho_m=rv * jnp.exp(G - Gm), b_hat=bv * e_l, k_hat=kv * e_l,
                        d_row=jnp.exp(GL), vm=vm))

    lo1 = lax.broadcasted_iota(jnp.int32, (L, LANE), 1) < N_C
    r4 = lax.broadcasted_iota(jnp.int32, (4 * L, LANE), 0)
    c4 = lax.broadcasted_iota(jnp.int32, (4 * L, LANE), 1) & (N_C - 1)
    tri4 = c4 < (r4 & (L - 1)) + jnp.where(r4 < 2 * L, 0, 1)
    rb = lax.broadcasted_iota(jnp.int32, (LANE, LANE), 0)
    cb = lax.broadcasted_iota(jnp.int32, (LANE, LANE), 1)
    bd = (rb // N_C) == (cb // N_C)
    eye = rb == cb
    zeros_l = jnp.zeros((L, LANE), BF16)

    units = [(ch, p) for ch in range(NCH) for p in range(N_PAIR)]
    nu = len(units)
    heads = [(n, hh) for n in range(nu) for hh in range(2)]
    sls = [slice(p * LANE, (p + 1) * LANE) for p in range(N_PAIR)]
    get = lambda name, u: ops[u[0]][name][:, sls[u[1]]]
    vms = [get('vm', u).astype(BF16) for u in units]
    a_cat = []
    for u in units:
        alm_p, rhm_p = get('al_m', u), get('rho_m', u)
        lhs4 = jnp.concatenate([jnp.where(lo1, alm_p, 0.0), jnp.where(lo1, 0.0, alm_p),
                                jnp.where(lo1, rhm_p, 0.0), jnp.where(lo1, 0.0, rhm_p)], axis=0)
        rhs2 = jnp.concatenate([get('be_m', u), get('ka_m', u)], axis=0)
        a_cat.append(jnp.where(tri4, _mm_nt(lhs4, rhs2), 0.0).astype(BF16))
    zv = [jnp.concatenate([zeros_l, v], axis=0) for v in vms]
    akv = [_mm(jnp.where(lo1, 0.0, a_cat[n][hh * L:(hh + 1) * L]), zv[n]) for n, hh in heads]
    xs, apow = [], []
    for n, u in enumerate(units):
        al_p = get('al', u)
        akv_p = jnp.where(lo1, akv[2 * n], akv[2 * n + 1])
        xs.append(jnp.where(lo1, al_p, pltpu.roll(akv_p, N_C, 1)))
        xs.append(jnp.where(lo1, pltpu.roll(al_p, N_C, 1), akv_p))
        for hh in range(2):
            apow.append(jnp.where(lo1, a_cat[n][hh * L:(hh + 1) * L], 0.0).astype(BF16))
    for step in range(6):
        if step < 5:
            res = [_mm(a[:, :N_C], jnp.concatenate([x.astype(BF16), a], axis=1)) for a, x in zip(apow, xs)]
            apow = [r[:, LANE:].astype(BF16) for r in res]
        else:
            res = [_mm(a[:, :N_C], x) for a, x in zip(apow, xs)]
        xs = [x + r[:, :LANE] for x, r in zip(xs, res)]
    w_pq = []
    for n in range(nu):
        x0, x1 = xs[2 * n], xs[2 * n + 1]
        p_pair = jnp.where(lo1, x0, pltpu.roll(x1, N_C, 1)).astype(BF16)
        q_pair = jnp.where(lo1, pltpu.roll(x0, N_C, 1), x1).astype(BF16)
        w_pq.append(jnp.concatenate([jnp.concatenate([p_pair, q_pair], axis=1),
                                     jnp.concatenate([zeros_l, vms[n]], axis=1)], axis=0))
    rbs = [_mm(a_cat[n][(2 + hh) * L:(3 + hh) * L], w_pq[n]) for n, hh in heads]
    gs = [_mm_tn(jnp.concatenate([get('b_hat', u), get('k_hat', u)], axis=0), w_pq[n])
          for n, u in enumerate(units)]
    for n, (ch, p) in enumerate(units):
        sl = sls[p]
        rows = slice(ch * L, (ch + 1) * L)
        r0, r1 = rbs[2 * n], rbs[2 * n + 1]
        rh_ref[rows, sl] = (get('rho', (ch, p)) + jnp.where(lo1, r0[:, :LANE], r1[:, :LANE])).astype(rh_ref.dtype)
        yl_ref[rows, sl] = jnp.where(lo1, r0[:, LANE:], r1[:, LANE:]).astype(yl_ref.dtype)
        d_p = jnp.broadcast_to(ops[ch]['d_row'][:, sl], (LANE, LANE))
        mt_ref[ch, p] = (jnp.where(eye, d_p, 0.0) + jnp.where(bd, gs[n][:, :LANE], 0.0)).astype(mt_ref.dtype)
        nt_ref[ch, p] = jnp.where(bd, gs[n][:, LANE:], 0.0)


def _rwkv_chunks(P, mu3, muwa, w0, wup, a0, aup, k_k, k_a, r_k):
    S = P.shape[0]
    nc = S // CHUNK
    nb = D_BR // LANE
    T = RW_NCH * CHUNK
    prev = lambda c: jnp.maximum(c * (T // SUB) - 1, 0)
    row = lambda n: pl.BlockSpec((1, n), lambda c: (0, 0))
    big = pl.BlockSpec((T, D_BR), lambda c: (c, 0))
    mat = pl.BlockSpec((RW_NCH, N_PAIR, LANE, LANE), lambda c: (c, 0, 0, 0))
    return pl.pallas_call(
        functools.partial(_rwkv_chunk_kernel, NCH=RW_NCH),
        grid=(nc // RW_NCH,),
        in_specs=[pl.BlockSpec((T, D_BR), lambda c: (c, PC_CR // nb)),
                  pl.BlockSpec((T, D_BR), lambda c: (c, PC_CK // nb)),
                  pl.BlockSpec((T, D_BR), lambda c: (c, PC_CV // nb)),
                  pl.BlockSpec((T, LANE), lambda c: (c, PC_WA)),
                  pl.BlockSpec((SUB, D_BR), lambda c: (prev(c), PC_CR // nb)),
                  pl.BlockSpec((SUB, D_BR), lambda c: (prev(c), PC_CK // nb)),
                  pl.BlockSpec((SUB, D_BR), lambda c: (prev(c), PC_CV // nb)),
                  pl.BlockSpec((SUB, LANE), lambda c: (prev(c), PC_WA)),
                  pl.BlockSpec((3, D_BR), lambda c: (0, 0)),
                  row(LANE), row(D_BR),
                  pl.BlockSpec((LANE, D_BR), lambda c: (0, 0)),
                  row(D_BR),
                  pl.BlockSpec((LANE, D_BR), lambda c: (0, 0)),
                  row(D_BR), row(D_BR), row(D_BR)],
        out_specs=[mat, mat, big, big, big],
        out_shape=[jax.ShapeDtypeStruct((nc, N_PAIR, LANE, LANE), BF16),
                   jax.ShapeDtypeStruct((nc, N_PAIR, LANE, LANE), F32),
                   jax.ShapeDtypeStruct((S, D_BR), BF16),
                   jax.ShapeDtypeStruct((S, D_BR), BF16),
                   jax.ShapeDtypeStruct((S, D_BR), BF16)],
        compiler_params=_cparams(("parallel",)),
        name="rwkv_chunks",
    )(P, P, P, P, P, P, P, P, mu3, muwa, w0, wup, a0, aup, k_k, k_a, r_k)


def _rwkv_seq_kernel(mt_ref, nt_ref, rh_ref, yl_ref, bonus_ref, gc_ref,
                     lnw_ref, lnb_ref, o_ref, z_ref, y_ref, *, NCH):
    i = pl.program_id(0)

    @pl.when(i == 0)
    def _():
        z_ref[...] = jnp.zeros_like(z_ref)

    zs = [z_ref[p] for p in range(N_PAIR)]
    for c in range(NCH):
        rows = slice(c * CHUNK, (c + 1) * CHUNK)
        zb = [z.astype(BF16) for z in zs]
        zs = [jnp.dot(mt_ref[c, p], zb[p], preferred_element_type=F32) + nt_ref[c, p]
              for p in range(N_PAIR)]
        for p in range(N_PAIR):
            sl = slice(p * LANE, (p + 1) * LANE)
            y_ref[rows, sl] = jnp.dot(rh_ref[rows, sl], zb[p], preferred_element_type=F32) + yl_ref[rows, sl]
    for p in range(N_PAIR):
        z_ref[p] = zs[p]

    y = y_ref[...]
    inv_n = 1.0 / N_C
    mu = _head_sum(y) * inv_n
    yc = y - mu
    var = _head_sum(yc * yc) * inv_n
    yn = yc * lax.rsqrt(var + GN_EPS) * lnw_ref[...] + lnb_ref[...]
    o_ref[...] = ((yn + bonus_ref[...]) * _silu(gc_ref[...])).astype(o_ref.dtype)


def _rwkv_seq(mt, nt, rh, yl, bonus, P, ln_w, ln_b):
    S = rh.shape[0]
    T = min(512, S)
    nch = T // CHUNK
    nb = D_BR // LANE
    big = pl.BlockSpec((T, D_BR), lambda i: (i, 0))
    mat = pl.BlockSpec((nch, N_PAIR, LANE, LANE), lambda i: (i, 0, 0, 0))
    row = pl.BlockSpec((1, D_BR), lambda i: (0, 0))
    return pl.pallas_call(
        functools.partial(_rwkv_seq_kernel, NCH=nch),
        grid=(S // T,),
        in_specs=[mat, mat, big, big, big,
                  pl.BlockSpec((T, D_BR), lambda i: (i, PC_GC // nb)),
                  row, row],
        out_specs=big,
        out_shape=jax.ShapeDtypeStruct((S, D_BR), BF16),
        scratch_shapes=[pltpu.VMEM((N_PAIR, LANE, LANE), F32),
                        pltpu.VMEM((T, D_BR), F32)],
        compiler_params=_cparams(("arbitrary",)),
        name="rwkv_seq",
    )(mt, nt, rh, yl, bonus, P, ln_w, ln_b)


def _merge_kernel(h_ref, wg0_ref, wg1_ref, wg2_ref, bm_ref, ya_ref, yb_ref, yc_ref,
                  wb0_ref, wb1_ref, wb2_ref, o_ref):
    h = h_ref[...]
    bm = bm_ref[...]
    acc = None
    for n, (wg_ref, y_ref, wb_ref) in enumerate(((wg0_ref, ya_ref, wb0_ref),
                                                  (wg1_ref, yb_ref, wb1_ref),
                                                  (wg2_ref, yc_ref, wb2_ref))):
        gate = _sigmoid(jnp.dot(h, wg_ref[...], preferred_element_type=F32) + bm[n:n + 1, :])
        term = gate * jnp.dot(y_ref[...], wb_ref[...], preferred_element_type=F32)
        acc = term if acc is None else acc + term
    o_ref[...] = acc.astype(o_ref.dtype)


def _merge(h, w_all, layer, b_merge, ya, yb, yc, w_br):
    S = h.shape[0]
    tm = min(1024, S)
    tn = 512
    nj = D_MODEL // tn
    ysp = pl.BlockSpec((tm, D_BR), lambda i, j: (i, 0))
    wgate = lambda n: pl.BlockSpec((None, D_MODEL, tn), lambda i, j: (layer, 0, n * nj + j))
    return pl.pallas_call(
        _merge_kernel,
        grid=(S // tm, nj),
        in_specs=[pl.BlockSpec((tm, D_MODEL), lambda i, j: (i, 0)),
                  wgate(0), wgate(1), wgate(2),
                  pl.BlockSpec((3, tn), lambda i, j: (0, j)),
                  ysp, ysp, ysp,
                  pl.BlockSpec((None, D_BR, tn), lambda i, j: (0, 0, j)),
                  pl.BlockSpec((None, D_BR, tn), lambda i, j: (1, 0, j)),
                  pl.BlockSpec((None, D_BR, tn), lambda i, j: (2, 0, j))],
        out_specs=pl.BlockSpec((tm, tn), lambda i, j: (i, j)),
        out_shape=jax.ShapeDtypeStruct((S, D_MODEL), BF16),
        compiler_params=pltpu.CompilerParams(dimension_semantics=("parallel", "parallel"),
                                             vmem_limit_bytes=FOX_VMEM),
        name="merge",
    )(h, w_all, w_all, w_all, b_merge, ya, yb, yc, w_br, w_br, w_br)


def _out_kernel(m_ref, w_ref, x_ref, g_ref, o_ref):
    y = jnp.dot(m_ref[...], w_ref[...], preferred_element_type=F32)
    yn = y * lax.rsqrt(jnp.mean(y * y, axis=-1, keepdims=True) + NORM_EPS)
    o_ref[...] = x_ref[...] + yn * g_ref[...]


def _out_proj(m, w_out, x, g):
    S = m.shape[0]
    tm = min(512, S)
    blk = pl.BlockSpec((tm, D_MODEL), lambda i: (i, 0))
    return pl.pallas_call(
        _out_kernel,
        grid=(S // tm,),
        in_specs=[blk,
                  pl.BlockSpec((D_MODEL, D_MODEL), lambda i: (0, 0)),
                  blk,
                  pl.BlockSpec((1, D_MODEL), lambda i: (0, 0))],
        out_specs=blk,
        out_shape=jax.ShapeDtypeStruct((S, D_MODEL), F32),
        compiler_params=_cparams(("parallel",)),
        name="out_proj",
    )(m, w_out, x, g.reshape(1, D_MODEL))


def _pack_plan():
    xa, qkv, fl = 0, 2 * D_BR, 5 * D_BR
    gb = fl + H_B
    pc = gb + D_BR
    gc = pc + 3 * D_BR + 2 * LORA
    mg = gc + D_BR
    groups = [(mg, 3 * D_MODEL), (qkv, 3 * D_BR), (xa, 2 * D_BR), (gb, D_BR), (gc, D_BR),
              (pc, 3 * D_BR + 2 * LORA)]
    src, mode = [], []
    for col0, width in groups:
        for j in range(width // LANE):
            c = col0 + j * LANE
            src.append(c // LANE)
            mode.append(0 if c % LANE == 0 else 1)
            assert c % LANE in (0, H_B)
    src.append(fl // LANE)
    mode.append(2)
    n_pad = P_COLS // LANE - (len(src) - (3 * D_MODEL + 3 * D_BR) // LANE)
    src += [0] * n_pad
    mode += [3] * n_pad
    return src, mode


PACK_NB = 12


def _pack_kernel(src_ref, mode_ref, *refs):
    a_refs, b_refs, o_ref = refs[:PACK_NB], refs[PACK_NB:2 * PACK_NB], refs[2 * PACK_NB]
    for n in range(PACK_NB):
        mode = mode_ref[pl.program_id(1) * PACK_NB + n]
        a_ref, b_ref = a_refs[n], b_refs[n]
        cols = slice(n * LANE, (n + 1) * LANE)

        def emit(t, cols=cols):
            o_ref[:, cols] = t.T.astype(o_ref.dtype)

        @pl.when(mode == 0)
        def _():
            emit(a_ref[...])

        @pl.when(mode == 1)
        def _():
            emit(jnp.concatenate([a_ref[SUB:, :], b_ref[...]], axis=0))

        @pl.when(mode == 2)
        def _():
            a = a_ref[...]
            emit(jnp.where(lax.broadcasted_iota(jnp.int32, a.shape, 0) < H_B, a, 0.0))

        @pl.when(mode == 3)
        def _():
            o_ref[:, cols] = jnp.zeros((o_ref.shape[0], LANE), o_ref.dtype)


def _pack_w_in(w_in):
    depth, K, n_in = w_in.shape
    w_t = jnp.swapaxes(w_in, 1, 2)
    src, mode = _pack_plan()
    assert len(src) % PACK_NB == 0
    sub_per_blk = LANE // SUB
    last8 = n_in // SUB - 1

    def a_spec(n):
        return pl.BlockSpec((None, LANE, K), lambda l, ob, src_ref, mode_ref: (l, src_ref[ob * PACK_NB + n], 0))

    def b_spec(n):
        return pl.BlockSpec((None, SUB, K), lambda l, ob, src_ref, mode_ref:
                            (l, jnp.minimum((src_ref[ob * PACK_NB + n] + 1) * sub_per_blk, last8), 0))

    return pl.pallas_call(
        _pack_kernel,
        grid_spec=pltpu.PrefetchScalarGridSpec(
            num_scalar_prefetch=2,
            grid=(depth, len(src) // PACK_NB),
            in_specs=[a_spec(n) for n in range(PACK_NB)] + [b_spec(n) for n in range(PACK_NB)],
            out_specs=pl.BlockSpec((None, K, PACK_NB * LANE), lambda l, ob, src_ref, mode_ref: (l, 0, ob))),
        out_shape=jax.ShapeDtypeStruct((depth, K, len(src) * LANE), BF16),
        compiler_params=_cparams(("parallel", "parallel")),
        name="pack_w_in",
    )(jnp.asarray(src, jnp.int32), jnp.asarray(mode, jnp.int32), *([w_t] * (2 * PACK_NB)))


W_COL_QKV = 3 * D_MODEL
W_COL_P = 3 * D_MODEL + 3 * D_BR


def _layer_weights(lru_gate_w, rwkv_w_up, rwkv_a_up):
    eye = jnp.eye(H_A, dtype=F32)
    wg = jnp.einsum('ghij,hk->ghikj', lru_gate_w, eye).reshape(2, D_BR, D_BR)
    wg = jnp.concatenate([wg[0], wg[1]], axis=1).astype(BF16)
    z = jnp.zeros((LORA, D_BR), F32)
    wup = jnp.concatenate([rwkv_w_up, z], axis=0).astype(BF16)
    aup = jnp.concatenate([z, rwkv_a_up], axis=0).astype(BF16)
    return wg, wup, aup


def kernel(x, pre_norm_w, post_norm_w, w_in, b_merge, conv_w, conv_b, lru_gate_w, lru_gate_b, lru_lambda,
           fox_b_f, rwkv_mu, rwkv_w0, rwkv_w_up, rwkv_a0, rwkv_a_up, rwkv_k_k, rwkv_k_a, rwkv_r_k,
           rwkv_ln_w, rwkv_ln_b, w_branch, w_out):
    B, S, _ = x.shape
    assert B == 1 and S % CHUNK == 0
    depth = w_in.shape[0]
    w_all = _pack_w_in(w_in)
    xs = x.reshape(S, D_MODEL)
    ones_p = jnp.ones((1, P_COLS), F32)
    qscale = jnp.concatenate([jnp.full((1, D_BR), DH_B ** -0.5 * LOG2E, F32), jnp.ones((1, 2 * D_BR), F32)], axis=1)
    row = lambda a: a.reshape(1, -1)
    for l in range(depth):
        wg, wup, aup = _layer_weights(lru_gate_w[l], rwkv_w_up[l], rwkv_a_up[l])
        qkv, h = _norm_proj(xs, pre_norm_w[l], w_all, l, W_COL_QKV, qscale, BF16, 1024)
        P = _proj(h, w_all, l, W_COL_P, ones_p, F32, 1536)
        ya = _rglru(P, conv_w[l], row(conv_b[l]), wg, row(lru_gate_b[l]), row(lru_lambda[l]))
        bf_pad = jnp.concatenate([fox_b_f[l], jnp.zeros((LANE - H_B,), F32)]).reshape(1, LANE)
        kx, cs = _fox_c(P, bf_pad)
        yb = _fox(qkv, kx, cs, P)
        mu = rwkv_mu[l]
        mu3 = mu[:3 * D_BR].reshape(3, D_BR)
        muwa = mu[3 * D_BR:].reshape(1, LANE)
        mt, nt, rh, yl, bonus = _rwkv_chunks(P, mu3, muwa, row(rwkv_w0[l]), wup, row(rwkv_a0[l]), aup,
                                             row(rwkv_k_k[l]), row(rwkv_k_a[l]), row(rwkv_r_k[l]))
        yc = _rwkv_seq(mt, nt, rh, yl, bonus, P, row(rwkv_ln_w[l]), row(rwkv_ln_b[l]))
        m = _merge(h, w_all, l, b_merge[l], ya, yb, yc, w_branch[l].astype(BF16))
        xs = _out_proj(m, w_out[l].astype(BF16), xs, post_norm_w[l])
    return xs.reshape(B, S, D_MODEL)
```
